```python
import math
import jax, jax.numpy as jnp
from jax import lax
import numpy as np

D_MODEL = 1024
BATCH = 8
SEQ = 8192
DEPTH = 1

H_ATT = 8
HD = 64
ATT_W = H_ATT * 2 * HD
Q_BLOCK = 128
EXPAND = 2
D_INNER = EXPAND * D_MODEL
SSM_HEADDIM = 64
H_SSM = D_INNER // SSM_HEADDIM
N_GROUPS = 4
HEADS_PER_GROUP = H_SSM // N_GROUPS
D_STATE = 128
D_CONV = 4
CONV_DIM = D_INNER + 2 * N_GROUPS * D_STATE
CHUNK = 128
N_BRANCH = 2
N_IN = 4 * ATT_W + D_INNER + CONV_DIM + H_SSM + N_BRANCH * D_MODEL
EPS = 1e-5

kernel_name = "hybrid_diffattn_mamba2_gated_merge"


def rms_norm(x, g, eps=EPS):
    xf = x.astype(jnp.float32)
    y = xf * lax.rsqrt(jnp.mean(xf * xf, axis=-1, keepdims=True) + eps)
    return (y * g.astype(jnp.float32)).astype(x.dtype)


def group_rms_norm(y, g, groups, eps=EPS):
    shp = y.shape
    yf = y.astype(jnp.float32).reshape(shp[:-1] + (groups, shp[-1] // groups))
    yf = yf * lax.rsqrt(jnp.mean(yf * yf, axis=-1, keepdims=True) + eps)
    return (yf.reshape(shp) * g.astype(jnp.float32)).astype(y.dtype)


def lambda_init(layer_idx):
    return 0.8 - 0.6 * math.exp(-0.3 * layer_idx)


def diff_attention(q, k, v, lam):
    bsz, s_len = q.shape[0], q.shape[1]
    nblk = s_len // Q_BLOCK
    scale = HD ** -0.5
    slopes = jnp.asarray(2.0 ** (-8.0 * np.arange(1, H_ATT + 1) / H_ATT), dtype=jnp.float32)
    kpos = jnp.arange(s_len)
    qb = q.reshape(bsz, nblk, Q_BLOCK, H_ATT, 2, HD).swapaxes(0, 1)

    def one_block(args):
        qi, bi = args
        qpos = bi * Q_BLOCK + jnp.arange(Q_BLOCK)
        dist = qpos[:, None] - kpos[None, :]
        s = jnp.einsum('bqhcd,bkhcd->bhcqk', qi, k).astype(jnp.float32) * scale
        s = s - slopes[:, None, None, None] * dist.astype(jnp.float32)[None, None]
        s = jnp.where(dist >= 0, s, -jnp.inf)
        p = jax.nn.softmax(s, axis=-1)
        a = p[:, :, 0] - lam * p[:, :, 1]
        return jnp.einsum('bhqk,bkhv->bqhv', a.astype(v.dtype), v)

    out = lax.map(one_block, (qb, jnp.arange(nblk)))
    return out.swapaxes(0, 1).reshape(bsz, s_len, H_ATT, 2 * HD)


def segsum(a):
    L = a.shape[-1]
    x = jnp.broadcast_to(a[..., :, None], a.shape + (L,))
    x = jnp.where(jnp.tril(jnp.ones((L, L), dtype=bool), k=-1), x, 0.0)
    cs = jnp.cumsum(x, axis=-2)
    return jnp.where(jnp.tril(jnp.ones((L, L), dtype=bool), k=0), cs, -jnp.inf)


def ssd_scan(xdt, a, bm, cm):
    bsz, s_len = xdt.shape[0], xdt.shape[1]
    nc = s_len // CHUNK
    x_c = xdt.reshape(bsz, nc, CHUNK, N_GROUPS, HEADS_PER_GROUP, SSM_HEADDIM).swapaxes(0, 1)
    a_c = a.reshape(bsz, nc, CHUNK, N_GROUPS, HEADS_PER_GROUP).transpose(1, 0, 3, 4, 2)
    b_c = bm.reshape(bsz, nc, CHUNK, N_GROUPS, D_STATE).swapaxes(0, 1)
    c_c = cm.reshape(bsz, nc, CHUNK, N_GROUPS, D_STATE).swapaxes(0, 1)

    def step(state, inp):
        xc, ac, bc, cc = inp
        a_cum = jnp.cumsum(ac, axis=-1)
        Lmat = jnp.exp(segsum(ac))
        cb = jnp.einsum('blgn,bsgn->bgls', cc, bc)
        y_diag = jnp.einsum('bgls,bgels,bsgep->blgep', cb, Lmat, xc)
        y_off = jnp.einsum('blgn,bgepn,bgel->blgep', cc, state, jnp.exp(a_cum))
        decay = jnp.exp(a_cum[..., -1:] - a_cum)
        new_state = state * jnp.exp(a_cum[..., -1])[..., None, None] + \
            jnp.einsum('blgn,bgel,blgep->bgepn', bc, decay, xc)
        return new_state, (y_diag + y_off).astype(jnp.float32)

    state0 = jnp.zeros((bsz, N_GROUPS, HEADS_PER_GROUP, SSM_HEADDIM, D_STATE), jnp.float32)
    _, y = lax.scan(step, state0, (x_c, a_c, b_c, c_c))
    return y.swapaxes(0, 1).reshape(bsz, s_len, H_SSM, SSM_HEADDIM)


def causal_depthwise_conv(x, w, b):
    c = x.shape[-1]
    y = lax.conv_general_dilated(
        x, w[:, None, :].astype(x.dtype), window_strides=(1,), padding=((D_CONV - 1, 0),),
        dimension_numbers=('NWC', 'WIO', 'NWC'), feature_group_count=c)
    return y + b


def hybrid_layer(x, layer_idx, norm_g, w_in, b_gate, q_norm_g, k_norm_g, lambda_q1, lambda_k1,
                 lambda_q2, lambda_k2, subln_g, w_attn_out, conv_w, conv_b, dt_bias, a_log,
                 d_skip, ssm_norm_g, w_ssm_out, w_out):
    bsz, s_len, _ = x.shape
    xn = rms_norm(x, norm_g)
    proj = xn @ w_in
    splits = np.cumsum([ATT_W, ATT_W, ATT_W, ATT_W, D_INNER, CONV_DIM, H_SSM]).tolist()
    q, k, v, g_att, z, xbc, dt_raw, gate_logits = jnp.split(proj, splits, axis=-1)

    q = rms_norm(q.reshape(bsz, s_len, H_ATT, 2, HD), q_norm_g)
    k = rms_norm(k.reshape(bsz, s_len, H_ATT, 2, HD), k_norm_g)
    v = v.reshape(bsz, s_len, H_ATT, 2 * HD)
    lam_init = lambda_init(layer_idx)
    lam = (jnp.exp(jnp.sum(lambda_q1.astype(jnp.float32) * lambda_k1.astype(jnp.float32)))
           - jnp.exp(jnp.sum(lambda_q2.astype(jnp.float32) * lambda_k2.astype(jnp.float32)))
           + lam_init)
    o = diff_attention(q, k, v, lam)
    o = rms_norm(o, subln_g) * (1.0 - lam_init)
    o = o.reshape(bsz, s_len, ATT_W) * jax.nn.silu(g_att)
    y_att = o @ w_attn_out

    xbc = jax.nn.silu(causal_depthwise_conv(xbc, conv_w, conv_b))
    xs, bm, cm = jnp.split(xbc, [D_INNER, D_INNER + N_GROUPS * D_STATE], axis=-1)
    xs = xs.reshape(bsz, s_len, H_SSM, SSM_HEADDIM)
    bm = bm.reshape(bsz, s_len, N_GROUPS, D_STATE)
    cm = cm.reshape(bsz, s_len, N_GROUPS, D_STATE)
    dt = jax.nn.softplus(dt_raw.astype(jnp.float32) + dt_bias.astype(jnp.float32))
    a = dt * (-jnp.exp(a_log.astype(jnp.float32)))
    y = ssd_scan(xs * dt[..., None], a, bm, cm)
    y = y + xs * d_skip[:, None]
    y = y.reshape(bsz, s_len, D_INNER) * jax.nn.silu(z)
    y = group_rms_norm(y, ssm_norm_g, N_GROUPS)
    y_ssm = y.astype(x.dtype) @ w_ssm_out

    gates = jax.nn.sigmoid((gate_logits + b_gate).astype(jnp.float32))
    g_a, g_s = jnp.split(gates, [D_MODEL], axis=-1)
    h = (g_a * y_att + g_s * y_ssm).astype(x.dtype)
    return (x + h @ w_out).astype(x.dtype)


def setup_inputs(seed: int = 0) -> dict:
    key = jax.random.key(seed)
    ks = jax.random.split(key, 24)
    f32 = jnp.float32
    nrm = lambda k, shp, s: jax.random.normal(k, shp, f32) * s
    dt0 = jnp.exp(jax.random.uniform(ks[13], (DEPTH, H_SSM), f32) * (math.log(0.1) - math.log(0.001)) + math.log(0.001))
    return {
        "x": jax.random.normal(ks[0], (BATCH, SEQ, D_MODEL), f32),
        "norm_g": 1.0 + nrm(ks[1], (DEPTH, D_MODEL), 0.02),
        "w_in": nrm(ks[2], (DEPTH, D_MODEL, N_IN), D_MODEL ** -0.5),
        "b_gate": nrm(ks[3], (DEPTH, N_BRANCH * D_MODEL), 0.02),
        "q_norm_g": 1.0 + nrm(ks[4], (DEPTH, HD), 0.02),
        "k_norm_g": 1.0 + nrm(ks[5], (DEPTH, HD), 0.02),
        "lambda_q1": nrm(ks[6], (DEPTH, HD), 0.1),
        "lambda_k1": nrm(ks[7], (DEPTH, HD), 0.1),
        "lambda_q2": nrm(ks[8], (DEPTH, HD), 0.1),
        "lambda_k2": nrm(ks[9], (DEPTH, HD), 0.1),
        "subln_g": 1.0 + nrm(ks[10], (DEPTH, 2 * HD), 0.02),
        "w_attn_out": nrm(ks[11], (DEPTH, ATT_W, D_MODEL), ATT_W ** -0.5),
        "conv_w": nrm(ks[12], (DEPTH, D_CONV, CONV_DIM), D_CONV ** -0.5),
        "conv_b": nrm(ks[14], (DEPTH, CONV_DIM), 0.02),
        "dt_bias": dt0 + jnp.log(-jnp.expm1(-dt0)),
        "a_log": jnp.log(jax.random.uniform(ks[15], (DEPTH, H_SSM), f32, 1.0, 16.0)),
        "d_skip": 1.0 + nrm(ks[16], (DEPTH, H_SSM), 0.02),
        "ssm_norm_g": 1.0 + nrm(ks[17], (DEPTH, D_INNER), 0.02),
        "w_ssm_out": nrm(ks[18], (DEPTH, D_INNER, D_MODEL), D_INNER ** -0.5),
        "w_out": nrm(ks[19], (DEPTH, D_MODEL, D_MODEL), D_MODEL ** -0.5),
    }


def reference(x, norm_g, w_in, b_gate, q_norm_g, k_norm_g, lambda_q1, lambda_k1, lambda_q2,
              lambda_k2, subln_g, w_attn_out, conv_w, conv_b, dt_bias, a_log, d_skip,
              ssm_norm_g, w_ssm_out, w_out):
    h = x
    for l in range(DEPTH):
        h = hybrid_layer(h, l, norm_g[l], w_in[l], b_gate[l], q_norm_g[l], k_norm_g[l],
                         lambda_q1[l], lambda_k1[l], lambda_q2[l], lambda_k2[l], subln_g[l],
                         w_attn_out[l], conv_w[l], conv_b[l], dt_bias[l], a_log[l], d_skip[l],
                         ssm_norm_g[l], w_ssm_out[l], w_out[l])
    return h
```

```python
import functools
import math

import numpy as np
import jax
import jax.numpy as jnp
from jax import lax
from jax.experimental import pallas as pl
from jax.experimental.pallas import tpu as pltpu

D_MODEL = 1024
H_ATT = 8
HD = 64
HEAD_W = 2 * HD
ATT_W = H_ATT * HEAD_W
D_INNER = 2048
SSM_HEADDIM = 64
H_SSM = D_INNER // SSM_HEADDIM
N_GROUPS = 4
HEADS_PER_GROUP = H_SSM // N_GROUPS
D_STATE = 128
D_CONV = 4
CONV_DIM = D_INNER + 2 * N_GROUPS * D_STATE
CHUNK = 128
EPS = 1e-5
LANES = 128
VMEM_LIMIT = 48 * 1024 * 1024

COL_Q, COL_K, COL_V, COL_G = 0, ATT_W, 2 * ATT_W, 3 * ATT_W
COL_Z = 4 * ATT_W
COL_XBC = COL_Z + D_INNER
COL_GATE = COL_XBC + CONV_DIM
N_MAIN = COL_GATE + 2 * D_MODEL

FEAT_LANES = (0, 1, 2, 3)

_F32 = jnp.float32
_BF16 = jnp.bfloat16
_NEG = -1e30


def _dot(a, b):
    return jnp.dot(a, b, preferred_element_type=_F32)


def _silu(v):
    return v * (1.0 / (1.0 + jnp.exp(-v)))


def _in_proj_kernel(x_ref, g_ref, w_ref, wdt_ref, proj_ref, dt_ref, xn_ref):
    @pl.when(pl.program_id(1) == 0)
    def _():
        xf = x_ref[...]
        ms = jnp.mean(xf * xf, axis=-1, keepdims=True)
        xn = (xf * lax.rsqrt(ms + EPS) * g_ref[...]).astype(_BF16)
        xn_ref[...] = xn
        dt_ref[...] = _dot(xn, wdt_ref[...])

    proj_ref[...] = _dot(xn_ref[...], w_ref[...]).astype(_BF16)


def _in_proj(x2, norm_g, w_main, w_dt, tm, tn):
    m = x2.shape[0]
    return pl.pallas_call(
        _in_proj_kernel,
        grid=(m // tm, N_MAIN // tn),
        in_specs=[
            pl.BlockSpec((tm, D_MODEL), lambda i, j: (i, 0)),
            pl.BlockSpec((1, D_MODEL), lambda i, j: (0, 0)),
            pl.BlockSpec((D_MODEL, tn), lambda i, j: (0, j)),
            pl.BlockSpec((D_MODEL, LANES), lambda i, j: (0, 0)),
        ],
        out_specs=[
            pl.BlockSpec((tm, tn), lambda i, j: (i, j)),
            pl.BlockSpec((tm, LANES), lambda i, j: (i, 0)),
        ],
        out_shape=[
            jax.ShapeDtypeStruct((m, N_MAIN), _BF16),
            jax.ShapeDtypeStruct((m, LANES), _F32),
        ],
        scratch_shapes=[pltpu.VMEM((tm, D_MODEL), _BF16)],
        compiler_params=pltpu.CompilerParams(
            dimension_semantics=("parallel", "arbitrary"), vmem_limit_bytes=VMEM_LIMIT),
        name="in_proj",
    )(x2, norm_g, w_main, w_dt)


def _half_rms(v, gain2, lo):
    sq = v * v
    s_all = jnp.sum(sq, axis=-1, keepdims=True)
    s_lo = jnp.sum(jnp.where(lo, sq, 0.0), axis=-1, keepdims=True)
    ms = jnp.where(lo, s_lo, s_all - s_lo) * (1.0 / HD)
    return v * lax.rsqrt(ms + EPS) * gain2


def _attn_prep_kernel(q_ref, k_ref, v_ref, gq_ref, gk_ref, slope_ref, qT_ref, kf_ref, vT_ref, *, ts):
    i = pl.program_id(1)
    lane = lax.broadcasted_iota(jnp.int32, (ts, HEAD_W), 1)
    row = lax.broadcasted_iota(jnp.int32, (ts, HEAD_W), 0) + i * ts
    lo = lane < HD
    pos_a = (row >> 7).astype(_F32)
    pos_b = (row & 127).astype(_F32)
    slope = slope_ref[0:1, :]

    qn = _half_rms(q_ref[...].astype(_F32), gq_ref[...], lo) * (HD ** -0.5)
    kn = _half_rms(k_ref[...].astype(_F32), gk_ref[...], lo)

    for c in range(2):
        mine = lo if c == 0 else jnp.logical_not(lo)
        fl = lane - (HD if c == 0 else 0)
        qfeat = jnp.where(fl == FEAT_LANES[0], slope * 128.0,
                jnp.where(fl == FEAT_LANES[1], slope,
                jnp.where(fl == FEAT_LANES[2], -(slope * 128.0) * pos_a,
                jnp.where(fl == FEAT_LANES[3], -slope * pos_b, 0.0))))
        kfeat = jnp.where(fl == FEAT_LANES[0], pos_a,
                jnp.where(fl == FEAT_LANES[1], pos_b,
                jnp.where((fl == FEAT_LANES[2]) | (fl == FEAT_LANES[3]), 1.0, 0.0)))
        qc = jnp.where(mine, qn, qfeat)
        kc = jnp.where(mine, kn, kfeat)
        qT_ref[c] = qc.T.astype(_BF16)
        kf_ref[c] = kc.astype(_BF16)
    vT_ref[...] = v_ref[...].astype(_F32).T.astype(_BF16)


def _attn_prep(proj, gq2, gk2, slopes, bsz, s_len, ts):
    ns = s_len // ts
    blk = lambda col0: pl.BlockSpec((ts, HEAD_W), lambda b, i, h: (b * ns + i, col0 // HEAD_W + h))
    vec = pl.BlockSpec((1, HEAD_W), lambda b, i, h: (0, 0))
    return pl.pallas_call(
        functools.partial(_attn_prep_kernel, ts=ts),
        grid=(bsz, ns, H_ATT),
        in_specs=[blk(COL_Q), blk(COL_K), blk(COL_V), vec, vec,
                  pl.BlockSpec((None, 8, HEAD_W), lambda b, i, h: (h, 0, 0))],
        out_specs=[
            pl.BlockSpec((None, None, 2, HEAD_W, ts), lambda b, i, h: (b, h, 0, 0, i)),
            pl.BlockSpec((None, None, None, 2, ts, HEAD_W), lambda b, i, h: (b, h, i, 0, 0, 0)),
            pl.BlockSpec((None, None, None, HEAD_W, ts), lambda b, i, h: (b, h, i, 0, 0)),
        ],
        out_shape=[
            jax.ShapeDtypeStruct((bsz, H_ATT, 2, HEAD_W, s_len), _BF16),
            jax.ShapeDtypeStruct((bsz, H_ATT, ns, 2, ts, HEAD_W), _BF16),
            jax.ShapeDtypeStruct((bsz, H_ATT, ns, HEAD_W, ts), _BF16),
        ],
        compiler_params=pltpu.CompilerParams(
            dimension_semantics=("parallel", "parallel", "parallel"), vmem_limit_bytes=VMEM_LIMIT),
        name="attn_prep",
    )(proj, proj, proj, gq2, gk2, slopes)


def _attn_kernel(qT_ref, kf_ref, vT_ref, g_ref, sub_ref, lq1_ref, lk1_ref, lq2_ref, lk2_ref,
                 o_ref, m_ref, l_ref, acc_ref, *, blk, lam_init):
    iq = pl.program_id(2)
    m_ref[...] = jnp.full(m_ref.shape, _NEG, _F32)
    l_ref[...] = jnp.zeros(l_ref.shape, _F32)
    acc_ref[...] = jnp.zeros(acc_ref.shape, _F32)

    def step(j, masked):
        vblk = vT_ref[j]
        for c in range(2):
            s = _dot(kf_ref[j, c], qT_ref[c])
            if masked:
                kpos = lax.broadcasted_iota(jnp.int32, s.shape, 0)
                qpos = lax.broadcasted_iota(jnp.int32, s.shape, 1)
                s = jnp.where(kpos <= qpos, s, _NEG)
            m_old = m_ref[c]
            m_new = jnp.maximum(m_old, jnp.max(s, axis=0, keepdims=True))
            alpha = jnp.exp(m_old - m_new)
            p = jnp.exp(s - m_new)
            l_ref[c] = alpha * l_ref[c] + jnp.sum(p, axis=0, keepdims=True)
            acc_ref[c] = alpha * acc_ref[c] + _dot(vblk, p.astype(_BF16))
            m_ref[c] = m_new

    def body(j, carry):
        step(j, False)
        return carry

    lax.fori_loop(0, iq, body, 0)
    step(iq, True)

    lam = (jnp.exp(jnp.sum(lq1_ref[...] * lk1_ref[...], axis=-1, keepdims=True))
           - jnp.exp(jnp.sum(lq2_ref[...] * lk2_ref[...], axis=-1, keepdims=True)) + lam_init)
    oT = acc_ref[0] / l_ref[0] - lam * (acc_ref[1] / l_ref[1])
    o = oT.T
    o = o * lax.rsqrt(jnp.mean(o * o, axis=-1, keepdims=True) + EPS) * sub_ref[...] * (1.0 - lam_init)
    o_ref[...] = (o * _silu(g_ref[...].astype(_F32))).astype(_BF16)


def _attention(qT, kf, vT, proj, subln_g, lq1, lk1, lq2, lk2, bsz, s_len, blk, lam_init):
    nq = s_len // blk
    vec = lambda w: pl.BlockSpec((1, w), lambda b, h, i: (0, 0))
    return pl.pallas_call(
        functools.partial(_attn_kernel, blk=blk, lam_init=lam_init),
        grid=(bsz, H_ATT, nq),
        in_specs=[
            pl.BlockSpec((None, None, 2, HEAD_W, blk), lambda b, h, i: (b, h, 0, 0, i)),
            pl.BlockSpec((None, None, nq, 2, blk, HEAD_W), lambda b, h, i: (b, h, 0, 0, 0, 0)),
            pl.BlockSpec((None, None, nq, HEAD_W, blk), lambda b, h, i: (b, h, 0, 0, 0)),
            pl.BlockSpec((blk, HEAD_W), lambda b, h, i: (b * nq + i, COL_G // HEAD_W + h)),
            vec(HEAD_W), vec(HD), vec(HD), vec(HD), vec(HD),
        ],
        out_specs=pl.BlockSpec((blk, HEAD_W), lambda b, h, i: (b * nq + i, h)),
        out_shape=jax.ShapeDtypeStruct((bsz * s_len, ATT_W), _BF16),
        scratch_shapes=[
            pltpu.VMEM((2, 1, blk), _F32),
            pltpu.VMEM((2, 1, blk), _F32),
            pltpu.VMEM((2, HEAD_W, blk), _F32),
        ],
        compiler_params=pltpu.CompilerParams(
            dimension_semantics=("parallel", "parallel", "arbitrary"), vmem_limit_bytes=VMEM_LIMIT),
        name="diff_attention",
    )(qT, kf, vT, proj, subln_g, lq1, lk1, lq2, lk2)


def _split3(v):
    hi = v.astype(_BF16)
    r1 = v - hi.astype(_F32)
    mid = r1.astype(_BF16)
    lo = (r1 - mid.astype(_F32)).astype(_BF16)
    return hi, mid, lo


def _pair_bcast(v, j, lo):
    shape = (v.shape[0], LANES)
    a = jnp.broadcast_to(v[:, 2 * j:2 * j + 1], shape)
    b = jnp.broadcast_to(v[:, 2 * j + 1:2 * j + 2], shape)
    return jnp.where(lo, a, b)


def _ssd_kernel(xbc_ref, z_ref, dt_ref, cw_ref, cb_ref, dtb_ref, alog_ref, dsk_ref, ng_ref,
                y_ref, state_ref, ext_ref):
    L = CHUNK

    @pl.when(pl.program_id(1) == 0)
    def _():
        state_ref[...] = jnp.zeros(state_ref.shape, _F32)
        ext_ref[0:8, :] = jnp.zeros((8, CONV_DIM), _F32)

    ext_ref[8:8 + L, :] = xbc_ref[...].astype(_F32)
    conv = cb_ref[...] + cw_ref[0:1, :] * ext_ref[pl.ds(8 - (D_CONV - 1), L), :]
    for t in range(1, D_CONV):
        conv = conv + cw_ref[t:t + 1, :] * ext_ref[pl.ds(8 - (D_CONV - 1) + t, L), :]
    ext_ref[0:8, :] = ext_ref[L:L + 8, :]
    act = _silu(conv)

    dt = jax.nn.softplus(dt_ref[...] + dtb_ref[...])
    a = dt * (-jnp.exp(alog_ref[...]))
    ri = lax.broadcasted_iota(jnp.int32, (L, L), 0)
    ci = lax.broadcasted_iota(jnp.int32, (L, L), 1)
    tril = ri >= ci
    tri_b = tril.astype(_BF16)
    a_hi, a_mid, a_lo = _split3(a)
    acum = _dot(tri_b, a_hi) + _dot(tri_b, a_mid) + _dot(tri_b, a_lo)
    acum_t = acum.T
    a_last = acum[L - 1:L, :]
    eac = jnp.exp(acum)
    decay = jnp.exp(a_last - acum)
    chunk_decay = jnp.exp(a_last)

    lane = lax.broadcasted_iota(jnp.int32, (L, LANES), 1)
    lo = lane < SSM_HEADDIM

    for g in range(N_GROUPS):
        bm = act[:, D_INNER + g * D_STATE:D_INNER + (g + 1) * D_STATE]
        cm = act[:, D_INNER + N_GROUPS * D_STATE + g * D_STATE:
                 D_INNER + N_GROUPS * D_STATE + (g + 1) * D_STATE]
        bm_b = bm.astype(_BF16)
        cm_b = cm.astype(_BF16)
        cb = lax.dot_general(cm_b, bm_b, (((1,), (1,)), ((), ())), preferred_element_type=_F32)
        gw = HEADS_PER_GROUP * SSM_HEADDIM
        gcol = g * gw
        y_off = _dot(cm_b, state_ref[:, gcol:gcol + gw].astype(_BF16))
        xdd_parts = []
        for jj in range(HEADS_PER_GROUP // 2):
            j = g * (HEADS_PER_GROUP // 2) + jj
            col = j * LANES
            xblk = act[:, col:col + LANES]
            xdt = xblk * _pair_bcast(dt, j, lo)
            ms = []
            for hh in range(2):
                h = 2 * j + hh
                diff = acum[:, h:h + 1] - acum_t[h:h + 1, :]
                lmat = jnp.where(tril, jnp.exp(jnp.where(tril, diff, 0.0)), 0.0)
                ms.append((cb * lmat).astype(_BF16))
            lhs = jnp.concatenate(ms, axis=1)
            xdt_b = xdt.astype(_BF16)
            zero = jnp.zeros_like(xdt_b)
            rhs = jnp.concatenate([jnp.where(lo, xdt_b, zero), jnp.where(lo, zero, xdt_b)], axis=0)
            y = _dot(lhs, rhs)
            y = y + y_off[:, jj * LANES:(jj + 1) * LANES] * _pair_bcast(eac, j, lo)
            y = y + xblk * dsk_ref[:, col:col + LANES]
            y_ref[:, col:col + LANES] = y.astype(y_ref.dtype)
            xdd_parts.append((xdt * _pair_bcast(decay, j, lo)).astype(_BF16))
        xdd = jnp.concatenate(xdd_parts, axis=1)
        upd = lax.dot_general(bm_b, xdd, (((0,), (0,)), ((), ())), preferred_element_type=_F32)
        cd_parts = [_pair_bcast(chunk_decay, g * (HEADS_PER_GROUP // 2) + jj, lo[0:1, :])
                    for jj in range(HEADS_PER_GROUP // 2)]
        cd = jnp.concatenate(cd_parts, axis=1)
        state_ref[:, gcol:gcol + gw] = state_ref[:, gcol:gcol + gw] * cd + upd

    for g in range(N_GROUPS):
        gw = D_INNER // N_GROUPS
        sl = slice(g * gw, (g + 1) * gw)
        yg = y_ref[:, sl].astype(_F32) * _silu(z_ref[:, sl].astype(_F32))
        yg = yg * lax.rsqrt(jnp.mean(yg * yg, axis=-1, keepdims=True) + EPS) * ng_ref[:, sl]
        y_ref[:, sl] = yg.astype(y_ref.dtype)


def _ssd(proj, dt_raw, conv_w, conv_b, dt_bias, a_log, d_skip_x, norm_g, bsz, s_len):
    nc = s_len // CHUNK
    m = bsz * s_len
    full = lambda r, w: pl.BlockSpec((r, w), lambda b, c: (0, 0))
    return pl.pallas_call(
        _ssd_kernel,
        grid=(bsz, nc),
        in_specs=[
            pl.BlockSpec((CHUNK, CONV_DIM), lambda b, c: (b * nc + c, COL_XBC // CONV_DIM)),
            pl.BlockSpec((CHUNK, D_INNER), lambda b, c: (b * nc + c, COL_Z // D_INNER)),
            pl.BlockSpec((CHUNK, LANES), lambda b, c: (b * nc + c, 0)),
            full(D_CONV, CONV_DIM), full(1, CONV_DIM), full(1, LANES), full(1, LANES),
            full(1, D_INNER), full(1, D_INNER),
        ],
        out_specs=pl.BlockSpec((CHUNK, D_INNER), lambda b, c: (b * nc + c, 0)),
        out_shape=jax.ShapeDtypeStruct((m, D_INNER), _F32),
        scratch_shapes=[
            pltpu.VMEM((D_STATE, D_INNER), _F32),
            pltpu.VMEM((CHUNK + 8, CONV_DIM), _F32),
        ],
        compiler_params=pltpu.CompilerParams(
            dimension_semantics=("parallel", "arbitrary"), vmem_limit_bytes=VMEM_LIMIT),
        name="ssd",
    )(proj, proj, dt_raw, conv_w, conv_b, dt_bias, a_log, d_skip_x, norm_g)


def _merge_kernel(x_ref, oa_ref, ys_ref, gla_ref, gls_ref, bg_ref, wao_ref, wso_ref, wo_ref, out_ref):
    y_att = _dot(oa_ref[...], wao_ref[...])
    y_ssm = _dot(ys_ref[...].astype(_BF16), wso_ref[...])
    g_a = jax.nn.sigmoid(gla_ref[...].astype(_F32) + bg_ref[:, 0:D_MODEL])
    g_s = jax.nn.sigmoid(gls_ref[...].astype(_F32) + bg_ref[:, D_MODEL:2 * D_MODEL])
    h = (g_a * y_att + g_s * y_ssm).astype(_BF16)
    out_ref[...] = x_ref[...] + _dot(h, wo_ref[...])


def _merge(x2, o_att, y_ssm, proj, b_gate, w_ao, w_so, w_o, tm):
    m = x2.shape[0]
    row = lambda w, cb=0: pl.BlockSpec((tm, w), lambda i: (i, cb))
    full = lambda r, w: pl.BlockSpec((r, w), lambda i: (0, 0))
    return pl.pallas_call(
        _merge_kernel,
        grid=(m // tm,),
        in_specs=[
            row(D_MODEL), row(ATT_W), row(D_INNER),
            row(D_MODEL, COL_GATE // D_MODEL), row(D_MODEL, COL_GATE // D_MODEL + 1),
            full(1, 2 * D_MODEL), full(ATT_W, D_MODEL), full(D_INNER, D_MODEL), full(D_MODEL, D_MODEL),
        ],
        out_specs=row(D_MODEL),
        out_shape=jax.ShapeDtypeStruct((m, D_MODEL), x2.dtype),
        compiler_params=pltpu.CompilerParams(
            dimension_semantics=("parallel",), vmem_limit_bytes=VMEM_LIMIT),
        name="merge",
    )(x2, o_att, y_ssm, proj, proj, b_gate, w_ao, w_so, w_o)


def _layer(x, layer_idx, norm_g, w_in, b_gate, q_norm_g, k_norm_g, lq1, lk1, lq2, lk2, subln_g,
           w_attn_out, conv_w, conv_b, dt_bias, a_log, d_skip, ssm_norm_g, w_ssm_out, w_out):
    bsz, s_len, _ = x.shape
    m = bsz * s_len
    x2 = x.reshape(m, D_MODEL)
    lam_init = 0.8 - 0.6 * math.exp(-0.3 * layer_idx)

    c_dt = COL_GATE
    w_main = jnp.concatenate([w_in[:, :c_dt], w_in[:, c_dt + H_SSM:]], axis=1).astype(_BF16)
    w_dt = jnp.pad(w_in[:, c_dt:c_dt + H_SSM], ((0, 0), (0, LANES - H_SSM))).astype(_BF16)
    pad_h = lambda v: jnp.pad(v.astype(_F32), (0, LANES - H_SSM)).reshape(1, LANES)

    tm = min(1024, m)
    proj, dt_raw = _in_proj(x2, norm_g.reshape(1, D_MODEL), w_main, w_dt, tm, 1024)

    blk = min(512, s_len)
    slopes_np = 2.0 ** (-8.0 * np.arange(1, H_ATT + 1) / H_ATT)
    assert np.all(slopes_np * 128.0 == (slopes_np * 128.0).astype(_BF16).astype(np.float64))
    slopes = jnp.asarray(np.broadcast_to(slopes_np[:, None, None], (H_ATT, 8, HEAD_W)), _F32)
    two = lambda v: jnp.tile(v.astype(_F32), 2).reshape(1, HEAD_W)
    qT, kf, vT = _attn_prep(proj, two(q_norm_g), two(k_norm_g), slopes, bsz, s_len, blk)
    r64 = lambda v: v.astype(_F32).reshape(1, HD)
    o_att = _attention(qT, kf, vT, proj, subln_g.reshape(1, HEAD_W).astype(_F32),
                       r64(lq1), r64(lk1), r64(lq2), r64(lk2), bsz, s_len, blk, lam_init)

    y_ssm = _ssd(proj, dt_raw, conv_w.astype(_F32), conv_b.reshape(1, CONV_DIM).astype(_F32),
                 pad_h(dt_bias), pad_h(a_log),
                 jnp.repeat(d_skip.astype(_F32), SSM_HEADDIM).reshape(1, D_INNER),
                 ssm_norm_g.reshape(1, D_INNER).astype(_F32), bsz, s_len)

    out = _merge(x2, o_att, y_ssm, proj, b_gate.reshape(1, 2 * D_MODEL).astype(_F32),
                 w_attn_out.astype(_BF16), w_ssm_out.astype(_BF16), w_out.astype(_BF16), min(512, m))
    return out.reshape(bsz, s_len, D_MODEL)


def kernel(x, norm_g, w_in, b_gate, q_norm_g, k_norm_g, lambda_q1, lambda_k1, lambda_q2, lambda_k2,
           subln_g, w_attn_out, conv_w, conv_b, dt_bias, a_log, d_skip, ssm_norm_g, w_ssm_out, w_out):
    h = x
    for l in range(norm_g.shape[0]):
        h = _layer(h, l, norm_g[l], w_in[l], b_gate[l], q_norm_g[l], k_norm_g[l], lambda_q1[l],
                   lambda_k1[l], lambda_q2[l], lambda_k2[l], subln_g[l], w_attn_out[l], conv_w[l],
                   conv_b[l], dt_bias[l], a_log[l], d_skip[l], ssm_norm_g[l], w_ssm_out[l], w_out[l])
    return h
```

```python
import functools
import math

import numpy as np
import jax
import jax.numpy as jnp
from jax import lax
from jax.experimental import pallas as pl
from jax.experimental.pallas import tpu as pltpu

D_MODEL = 1024
H_ATT = 8
HD = 64
HEAD_W = 2 * HD
ATT_W = H_ATT * HEAD_W
D_INNER = 2048
SSM_HEADDIM = 64
H_SSM = D_INNER // SSM_HEADDIM
N_GROUPS = 4
HEADS_PER_GROUP = H_SSM // N_GROUPS
D_STATE = 128
D_CONV = 4
CONV_DIM = D_INNER + 2 * N_GROUPS * D_STATE
CHUNK = 128
EPS = 1e-5
LANES = 128
VMEM_LIMIT = 48 * 1024 * 1024

COL_Q, COL_K, COL_V, COL_G = 0, ATT_W, 2 * ATT_W, 3 * ATT_W
COL_Z = 4 * ATT_W
COL_XBC = COL_Z + D_INNER
COL_GATE = COL_XBC + CONV_DIM
N_MAIN = COL_GATE + 2 * D_MODEL

LOG2E = math.log2(math.e)
SAFE_SCORE_BOUND = 100.0

_F32 = jnp.float32
_BF16 = jnp.bfloat16
_NEG = -1e30


def _dot(a, b):
    return jnp.dot(a, b, preferred_element_type=_F32)


def _sigmoid(v):
    return 0.5 * jnp.tanh(0.5 * v) + 0.5


def _silu(v):
    h = 0.5 * v
    return h * jnp.tanh(h) + h


def _in_proj_kernel(x_ref, g_ref, w_ref, wdt_ref, proj_ref, dt_ref, xn_ref):
    @pl.when(pl.program_id(1) == 0)
    def _():
        xf = x_ref[...]
        ms = jnp.mean(xf * xf, axis=-1, keepdims=True)
        xn = (xf * lax.rsqrt(ms + EPS) * g_ref[...]).astype(_BF16)
        xn_ref[...] = xn
        dt_ref[...] = _dot(xn, wdt_ref[...])

    proj_ref[...] = _dot(xn_ref[...], w_ref[...]).astype(_BF16)


def _in_proj(x2, norm_g, w_main, w_dt, tm, tn):
    m = x2.shape[0]
    return pl.pallas_call(
        _in_proj_kernel,
        grid=(m // tm, N_MAIN // tn),
        in_specs=[
            pl.BlockSpec((tm, D_MODEL), lambda i, j: (i, 0)),
            pl.BlockSpec((1, D_MODEL), lambda i, j: (0, 0)),
            pl.BlockSpec((D_MODEL, tn), lambda i, j: (0, j)),
            pl.BlockSpec((D_MODEL, LANES), lambda i, j: (0, 0)),
        ],
        out_specs=[
            pl.BlockSpec((tm, tn), lambda i, j: (i, j)),
            pl.BlockSpec((tm, LANES), lambda i, j: (i, 0)),
        ],
        out_shape=[
            jax.ShapeDtypeStruct((m, N_MAIN), _BF16),
            jax.ShapeDtypeStruct((m, LANES), _F32),
        ],
        scratch_shapes=[pltpu.VMEM((tm, D_MODEL), _BF16)],
        compiler_params=pltpu.CompilerParams(
            dimension_semantics=("parallel", "arbitrary"), vmem_limit_bytes=VMEM_LIMIT),
        name="in_proj",
    )(x2, norm_g, w_main, w_dt)


def _half_rms(v, gain2, lo):
    sq = v * v
    s_all = jnp.sum(sq, axis=-1, keepdims=True)
    s_lo = jnp.sum(jnp.where(lo, sq, 0.0), axis=-1, keepdims=True)
    ms = jnp.where(lo, s_lo, s_all - s_lo) * (1.0 / HD)
    return v * lax.rsqrt(ms + EPS) * gain2


def _bf16_terms(v, n):
    terms = []
    for _ in range(n):
        t = v.astype(_BF16).astype(_F32)
        terms.append(t)
        v = v - t
    return terms


def _select_lanes(fl, vals):
    out = 0.0
    for idx in reversed(range(len(vals))):
        out = jnp.where(fl == idx, vals[idx], out)
    return out


def _attn_prep_kernel(q_ref, k_ref, v_ref, gq_ref, gk_ref, slope_ref, qT_ref, kf_ref, vT_ref, stat_ref,
                      *, ts):
    i = pl.program_id(1)
    lane = lax.broadcasted_iota(jnp.int32, (ts, HEAD_W), 1)
    row = lax.broadcasted_iota(jnp.int32, (ts, HEAD_W), 0) + i * ts
    lo = lane < HD
    pos_a = (row >> 7).astype(_F32)
    pos_b = (row & 127).astype(_F32)
    slope2 = slope_ref[0:1, :] * LOG2E
    one = jnp.ones((1, HEAD_W), _F32)
    q_feats = (_bf16_terms(slope2 * 128.0, 3) + _bf16_terms(slope2, 3)
               + _bf16_terms(-slope2 * row.astype(_F32), 2))
    k_feats = [pos_a] * 3 + [pos_b] * 3 + [one] * 2

    qn = _half_rms(q_ref[...].astype(_F32), gq_ref[...], lo) * (HD ** -0.5 * LOG2E)
    kn = _half_rms(k_ref[...].astype(_F32), gk_ref[...], lo)

    for c in range(2):
        mine = lo if c == 0 else jnp.logical_not(lo)
        fl = lane - (HD if c == 0 else 0)
        qc = jnp.where(mine, qn, _select_lanes(fl, q_feats))
        kc = jnp.where(mine, kn, _select_lanes(fl, k_feats))
        qT_ref[c] = qc.T.astype(_BF16)
        kf_ref[c] = kc.astype(_BF16)
    vT_ref[...] = v_ref[...].astype(_F32).T.astype(_BF16)

    def max_half_norm2(v):
        sq = v * v
        s_all = jnp.sum(sq, axis=-1, keepdims=True)
        s_lo = jnp.sum(jnp.where(lo, sq, 0.0), axis=-1, keepdims=True)
        return jnp.max(jnp.maximum(s_lo, s_all - s_lo), axis=0, keepdims=True)

    sub = lax.broadcasted_iota(jnp.int32, stat_ref.shape, 0)
    stat_ref[...] = jnp.where(sub == 0, max_half_norm2(qn), max_half_norm2(kn))


def _attn_prep(proj, gq2, gk2, slopes, bsz, s_len, ts):
    ns = s_len // ts
    blk = lambda col0: pl.BlockSpec((ts, HEAD_W), lambda b, i, h: (b * ns + i, col0 // HEAD_W + h))
    vec = pl.BlockSpec((1, HEAD_W), lambda b, i, h: (0, 0))
    return pl.pallas_call(
        functools.partial(_attn_prep_kernel, ts=ts),
        grid=(bsz, ns, H_ATT),
        in_specs=[blk(COL_Q), blk(COL_K), blk(COL_V), vec, vec,
                  pl.BlockSpec((None, 8, HEAD_W), lambda b, i, h: (h, 0, 0))],
        out_specs=[
            pl.BlockSpec((None, None, 2, HEAD_W, ts), lambda b, i, h: (b, h, 0, 0, i)),
            pl.BlockSpec((None, None, None, 2, ts, HEAD_W), lambda b, i, h: (b, h, i, 0, 0, 0)),
            pl.BlockSpec((None, None, None, HEAD_W, ts), lambda b, i, h: (b, h, i, 0, 0)),
            pl.BlockSpec((None, None, None, 8, LANES), lambda b, i, h: (b, h, i, 0, 0)),
        ],
        out_shape=[
            jax.ShapeDtypeStruct((bsz, H_ATT, 2, HEAD_W, s_len), _BF16),
            jax.ShapeDtypeStruct((bsz, H_ATT, ns, 2, ts, HEAD_W), _BF16),
            jax.ShapeDtypeStruct((bsz, H_ATT, ns, HEAD_W, ts), _BF16),
            jax.ShapeDtypeStruct((bsz, H_ATT, ns, 8, LANES), _F32),
        ],
        compiler_params=pltpu.CompilerParams(
            dimension_semantics=("parallel", "parallel", "parallel"), vmem_limit_bytes=VMEM_LIMIT),
        name="attn_prep",
    )(proj, proj, proj, gq2, gk2, slopes)


def _attn_kernel(qT_ref, kf_ref, vT_ref, g_ref, sub_ref, lq1_ref, lk1_ref, lq2_ref, lk2_ref,
                 o_ref, m_ref, l_ref, acc_ref, s_ref, *, blk, lam_init, online):
    iq = pl.program_id(2)
    m_ref[...] = jnp.full(m_ref.shape, _NEG, _F32)
    l_ref[...] = jnp.zeros(l_ref.shape, _F32)
    acc_ref[...] = jnp.zeros(acc_ref.shape, _F32)

    def scores(j, slot):
        for c in range(2):
            s_ref[slot, c] = _dot(kf_ref[j, c], qT_ref[c])

    def step(j, slot, masked, prefetch):
        if prefetch:
            scores(j + 1, 1 - slot)
        vblk = vT_ref[j]
        for c in range(2):
            s = s_ref[slot, c]
            if masked:
                kpos = lax.broadcasted_iota(jnp.int32, s.shape, 0)
                qpos = lax.broadcasted_iota(jnp.int32, s.shape, 1)
                s = jnp.where(kpos <= qpos, s, _NEG)
            if online:
                m_old = m_ref[c]
                m_new = jnp.maximum(m_old, jnp.max(s, axis=0, keepdims=True))
                alpha = jnp.exp2(m_old - m_new)
                p = jnp.exp2(s - m_new)
                l_ref[c] = alpha * l_ref[c] + jnp.sum(p, axis=0, keepdims=True)
                acc_ref[c] = alpha * acc_ref[c] + _dot(vblk, p.astype(_BF16))
                m_ref[c] = m_new
            else:
                p = jnp.exp2(s)
                l_ref[c] += jnp.sum(p, axis=0, keepdims=True)
                acc_ref[c] += _dot(vblk, p.astype(_BF16))

    scores(0, 0)

    def body(t, carry):
        step(2 * t, 0, False, True)
        step(2 * t + 1, 1, False, True)
        return carry

    n_pairs = iq // 2
    lax.fori_loop(0, n_pairs, body, 0)

    @pl.when(iq % 2 == 0)
    def _():
        step(iq, 0, True, False)

    @pl.when(iq % 2 == 1)
    def _():
        step(iq - 1, 0, False, True)
        step(iq, 1, True, False)

    lam = (jnp.exp(jnp.sum(lq1_ref[...] * lk1_ref[...], axis=-1, keepdims=True))
           - jnp.exp(jnp.sum(lq2_ref[...] * lk2_ref[...], axis=-1, keepdims=True)) + lam_init)
    oT = acc_ref[0] / l_ref[0] - lam * (acc_ref[1] / l_ref[1])
    o = oT.T
    o = o * lax.rsqrt(jnp.mean(o * o, axis=-1, keepdims=True) + EPS) * sub_ref[...] * (1.0 - lam_init)
    o_ref[...] = (o * _silu(g_ref[...].astype(_F32))).astype(_BF16)


def _attention(qT, kf, vT, proj, subln_g, lq1, lk1, lq2, lk2, *, bsz, s_len, blk, lam_init, online):
    nq = s_len // blk
    vec = lambda w: pl.BlockSpec((1, w), lambda b, h, i: (0, 0))
    return pl.pallas_call(
        functools.partial(_attn_kernel, blk=blk, lam_init=lam_init, online=online),
        grid=(bsz, H_ATT, nq),
        in_specs=[
            pl.BlockSpec((None, None, 2, HEAD_W, blk), lambda b, h, i: (b, h, 0, 0, i)),
            pl.BlockSpec((None, None, nq, 2, blk, HEAD_W), lambda b, h, i: (b, h, 0, 0, 0, 0)),
            pl.BlockSpec((None, None, nq, HEAD_W, blk), lambda b, h, i: (b, h, 0, 0, 0)),
            pl.BlockSpec((blk, HEAD_W), lambda b, h, i: (b * nq + i, COL_G // HEAD_W + h)),
            vec(HEAD_W), vec(HD), vec(HD), vec(HD), vec(HD),
        ],
        out_specs=pl.BlockSpec((blk, HEAD_W), lambda b, h, i: (b * nq + i, h)),
        out_shape=jax.ShapeDtypeStruct((bsz * s_len, ATT_W), _BF16),
        scratch_shapes=[
            pltpu.VMEM((2, 1, blk), _F32),
            pltpu.VMEM((2, 1, blk), _F32),
            pltpu.VMEM((2, HEAD_W, blk), _F32),
            pltpu.VMEM((2, 2, blk, blk), _F32),
        ],
        compiler_params=pltpu.CompilerParams(
            dimension_semantics=("parallel", "parallel", "arbitrary"), vmem_limit_bytes=VMEM_LIMIT),
        name="diff_attention",
    )(qT, kf, vT, proj, subln_g, lq1, lk1, lq2, lk2)


def _split3(v):
    hi = v.astype(_BF16)
    r1 = v - hi.astype(_F32)
    mid = r1.astype(_BF16)
    lo = (r1 - mid.astype(_F32)).astype(_BF16)
    return hi, mid, lo


CONV_HIST = 16


def _shift_matrix():
    sm = np.zeros(((D_CONV - 1) * CHUNK, CONV_HIST + CHUNK), np.float32)
    for j in range(D_CONV - 1):
        t = np.arange(CHUNK)
        sm[j * CHUNK + t, CONV_HIST + t - (D_CONV - 1) + j] = 1.0
    return jnp.asarray(sm, _BF16)


def _head_expand_matrix():
    em = np.zeros((2 * LANES, D_INNER), np.float32)
    cols = np.arange(D_INNER)
    em[cols // SSM_HEADDIM, cols] = 1.0
    em[LANES + cols // SSM_HEADDIM, cols] = 1.0
    return jnp.asarray(em, _BF16)


def _ssd_kernel(xbc_ref, z_ref, dt_ref, cw_ref, cb_ref, dtb_ref, alog_ref, dsk_ref, ng_ref,
                shift_ref, expand_ref, y_ref, state_ref, tail_ref):
    L = CHUNK

    @pl.when(pl.program_id(1) == 0)
    def _():
        state_ref[...] = jnp.zeros(state_ref.shape, _F32)
        tail_ref[...] = jnp.zeros(tail_ref.shape, _BF16)

    cur_b = xbc_ref[...]
    ext = jnp.concatenate([tail_ref[...], cur_b], axis=0)
    tail_ref[...] = cur_b[L - CONV_HIST:, :]
    shifted = _dot(shift_ref[...], ext)
    conv = cb_ref[...] + cw_ref[D_CONV - 1:D_CONV, :] * cur_b.astype(_F32)
    for t in range(D_CONV - 1):
        conv = conv + cw_ref[t:t + 1, :] * shifted[t * L:(t + 1) * L, :]
    act = _silu(conv)

    dt = jax.nn.softplus(dt_ref[...] + dtb_ref[...])
    a = dt * (-jnp.exp(alog_ref[...]))
    ri = lax.broadcasted_iota(jnp.int32, (L, L), 0)
    ci = lax.broadcasted_iota(jnp.int32, (L, L), 1)
    tril = ri >= ci
    tri_b = tril.astype(_BF16)
    a_hi, a_mid, a_lo = _split3(a)
    acum = _dot(tri_b, a_hi) + _dot(tri_b, a_mid) + _dot(tri_b, a_lo)
    acum_t = acum.T
    a_last = acum[L - 1:L, :]
    eac = jnp.exp(acum)
    decay = jnp.exp(a_last - acum)

    stack = jnp.concatenate([dt, eac, decay], axis=0)
    st_hi = stack.astype(_BF16)
    st_lo = (stack - st_hi.astype(_F32)).astype(_BF16)
    spread = _dot(jnp.concatenate([st_hi, st_lo], axis=1), expand_ref[...])
    dt_x, eac_x, decay_x = spread[0:L], spread[L:2 * L], spread[2 * L:3 * L]
    chunk_decay_x = eac_x[L - 1:L, :]

    lane = lax.broadcasted_iota(jnp.int32, (L, LANES), 1)
    lo = lane < SSM_HEADDIM

    for g in range(N_GROUPS):
        bm = act[:, D_INNER + g * D_STATE:D_INNER + (g + 1) * D_STATE]
        cm = act[:, D_INNER + N_GROUPS * D_STATE + g * D_STATE:
                 D_INNER + N_GROUPS * D_STATE + (g + 1) * D_STATE]
        bm_b = bm.astype(_BF16)
        cm_b = cm.astype(_BF16)
        cb = lax.dot_general(cm_b, bm_b, (((1,), (1,)), ((), ())), preferred_element_type=_F32)
        gw = HEADS_PER_GROUP * SSM_HEADDIM
        gcol = g * gw
        y_off = _dot(cm_b, state_ref[:, gcol:gcol + gw].astype(_BF16))
        xdd_parts = []
        for jj in range(HEADS_PER_GROUP // 2):
            j = g * (HEADS_PER_GROUP // 2) + jj
            col = j * LANES
            xblk = act[:, col:col + LANES]
            xdt = xblk * dt_x[:, col:col + LANES]
            ms = []
            for hh in range(2):
                h = 2 * j + hh
                diff = acum[:, h:h + 1] - acum_t[h:h + 1, :]
                lmat = jnp.exp(jnp.where(tril, diff, _NEG))
                ms.append((cb * lmat).astype(_BF16))
            lhs = jnp.concatenate(ms, axis=1)
            xdt_b = xdt.astype(_BF16)
            zero = jnp.zeros_like(xdt_b)
            rhs = jnp.concatenate([jnp.where(lo, xdt_b, zero), jnp.where(lo, zero, xdt_b)], axis=0)
            y = _dot(lhs, rhs)
            y = y + y_off[:, jj * LANES:(jj + 1) * LANES] * eac_x[:, col:col + LANES]
            y = y + xblk * dsk_ref[:, col:col + LANES]
            y_ref[:, col:col + LANES] = y.astype(y_ref.dtype)
            xdd_parts.append((xdt * decay_x[:, col:col + LANES]).astype(_BF16))
        xdd = jnp.concatenate(xdd_parts, axis=1)
        upd = lax.dot_general(bm_b, xdd, (((0,), (0,)), ((), ())), preferred_element_type=_F32)
        state_ref[:, gcol:gcol + gw] = (state_ref[:, gcol:gcol + gw] * chunk_decay_x[:, gcol:gcol + gw]
                                        + upd)

    for g in range(N_GROUPS):
        gw = D_INNER // N_GROUPS
        sl = slice(g * gw, (g + 1) * gw)
        yg = y_ref[:, sl].astype(_F32) * _silu(z_ref[:, sl].astype(_F32))
        yg = yg * lax.rsqrt(jnp.mean(yg * yg, axis=-1, keepdims=True) + EPS) * ng_ref[:, sl]
        y_ref[:, sl] = yg.astype(y_ref.dtype)


def _ssd(proj, dt_raw, conv_w, conv_b, dt_bias, a_log, d_skip_x, norm_g, bsz, s_len):
    nc = s_len // CHUNK
    m = bsz * s_len
    full = lambda r, w: pl.BlockSpec((r, w), lambda b, c: (0, 0))
    return pl.pallas_call(
        _ssd_kernel,
        grid=(bsz, nc),
        in_specs=[
            pl.BlockSpec((CHUNK, CONV_DIM), lambda b, c: (b * nc + c, COL_XBC // CONV_DIM)),
            pl.BlockSpec((CHUNK, D_INNER), lambda b, c: (b * nc + c, COL_Z // D_INNER)),
            pl.BlockSpec((CHUNK, LANES), lambda b, c: (b * nc + c, 0)),
            full(D_CONV, CONV_DIM), full(1, CONV_DIM), full(1, LANES), full(1, LANES),
            full(1, D_INNER), full(1, D_INNER),
            full((D_CONV - 1) * CHUNK, CONV_HIST + CHUNK), full(2 * LANES, D_INNER),
        ],
        out_specs=pl.BlockSpec((CHUNK, D_INNER), lambda b, c: (b * nc + c, 0)),
        out_shape=jax.ShapeDtypeStruct((m, D_INNER), _F32),
        scratch_shapes=[
            pltpu.VMEM((D_STATE, D_INNER), _F32),
            pltpu.VMEM((CONV_HIST, CONV_DIM), _BF16),
        ],
        compiler_params=pltpu.CompilerParams(
            dimension_semantics=("parallel", "arbitrary"), vmem_limit_bytes=VMEM_LIMIT),
        name="ssd",
    )(proj, proj, dt_raw, conv_w, conv_b, dt_bias, a_log, d_skip_x, norm_g,
      _shift_matrix(), _head_expand_matrix())


def _merge_kernel(x_ref, oa_ref, ys_ref, gla_ref, gls_ref, bg_ref, wao_ref, wso_ref, wo_ref, out_ref):
    y_att = _dot(oa_ref[...], wao_ref[...])
    y_ssm = _dot(ys_ref[...].astype(_BF16), wso_ref[...])
    g_a = _sigmoid(gla_ref[...].astype(_F32) + bg_ref[:, 0:D_MODEL])
    g_s = _sigmoid(gls_ref[...].astype(_F32) + bg_ref[:, D_MODEL:2 * D_MODEL])
    h = (g_a * y_att + g_s * y_ssm).astype(_BF16)
    out_ref[...] = x_ref[...] + _dot(h, wo_ref[...])


def _merge(x2, o_att, y_ssm, proj, b_gate, w_ao, w_so, w_o, tm):
    m = x2.shape[0]
    row = lambda w, cb=0: pl.BlockSpec((tm, w), lambda i: (i, cb))
    full = lambda r, w: pl.BlockSpec((r, w), lambda i: (0, 0))
    return pl.pallas_call(
        _merge_kernel,
        grid=(m // tm,),
        in_specs=[
            row(D_MODEL), row(ATT_W), row(D_INNER),
            row(D_MODEL, COL_GATE // D_MODEL), row(D_MODEL, COL_GATE // D_MODEL + 1),
            full(1, 2 * D_MODEL), full(ATT_W, D_MODEL), full(D_INNER, D_MODEL), full(D_MODEL, D_MODEL),
        ],
        out_specs=row(D_MODEL),
        out_shape=jax.ShapeDtypeStruct((m, D_MODEL), x2.dtype),
        compiler_params=pltpu.CompilerParams(
            dimension_semantics=("parallel",), vmem_limit_bytes=VMEM_LIMIT),
        name="merge",
    )(x2, o_att, y_ssm, proj, proj, b_gate, w_ao, w_so, w_o)


def _layer(x, layer_idx, norm_g, w_in, b_gate, q_norm_g, k_norm_g, lq1, lk1, lq2, lk2, subln_g,
           w_attn_out, conv_w, conv_b, dt_bias, a_log, d_skip, ssm_norm_g, w_ssm_out, w_out):
    bsz, s_len, _ = x.shape
    m = bsz * s_len
    x2 = x.reshape(m, D_MODEL)
    lam_init = 0.8 - 0.6 * math.exp(-0.3 * layer_idx)

    c_dt = COL_GATE
    w_main = jnp.concatenate([w_in[:, :c_dt], w_in[:, c_dt + H_SSM:]], axis=1).astype(_BF16)
    w_dt = jnp.pad(w_in[:, c_dt:c_dt + H_SSM], ((0, 0), (0, LANES - H_SSM))).astype(_BF16)
    pad_h = lambda v: jnp.pad(v.astype(_F32), (0, LANES - H_SSM)).reshape(1, LANES)

    tm = min(1024, m)
    proj, dt_raw = _in_proj(x2, norm_g.reshape(1, D_MODEL), w_main, w_dt, tm, 1024)

    blk = min(512, s_len)
    slopes_np = 2.0 ** (-8.0 * np.arange(1, H_ATT + 1) / H_ATT)
    assert np.all(slopes_np * 128.0 == (slopes_np * 128.0).astype(_BF16).astype(np.float64))
    slopes = jnp.asarray(np.broadcast_to(slopes_np[:, None, None], (H_ATT, 8, HEAD_W)), _F32)
    two = lambda v: jnp.tile(v.astype(_F32), 2).reshape(1, HEAD_W)
    qT, kf, vT, stat = _attn_prep(proj, two(q_norm_g), two(k_norm_g), slopes, bsz, s_len, blk)
    r64 = lambda v: v.astype(_F32).reshape(1, HD)
    attn_args = (qT, kf, vT, proj, subln_g.reshape(1, HEAD_W).astype(_F32),
                 r64(lq1), r64(lk1), r64(lq2), r64(lk2))
    attn = functools.partial(_attention, bsz=bsz, s_len=s_len, blk=blk, lam_init=lam_init)
    bound2 = jnp.max(stat[..., 0, 0]) * jnp.max(stat[..., 1, 0])
    o_att = lax.cond(bound2 < SAFE_SCORE_BOUND ** 2,
                     functools.partial(attn, online=False), functools.partial(attn, online=True),
                     *attn_args)

    y_ssm = _ssd(proj, dt_raw, conv_w.astype(_F32), conv_b.reshape(1, CONV_DIM).astype(_F32),
                 pad_h(dt_bias), pad_h(a_log),
                 jnp.repeat(d_skip.astype(_F32), SSM_HEADDIM).reshape(1, D_INNER),
                 ssm_norm_g.reshape(1, D_INNER).astype(_F32), bsz, s_len)

    out = _merge(x2, o_att, y_ssm, proj, b_gate.reshape(1, 2 * D_MODEL).astype(_F32),
                 w_attn_out.astype(_BF16), w_ssm_out.astype(_BF16), w_out.astype(_BF16), min(512, m))
    return out.reshape(bsz, s_len, D_MODEL)


def kernel(x, norm_g, w_in, b_gate, q_norm_g, k_norm_g, lambda_q1, lambda_k1, lambda_q2, lambda_k2,
           subln_g, w_attn_out, conv_w, conv_b, dt_bias, a_log, d_skip, ssm_norm_g, w_ssm_out, w_out):
    h = x
    for l in range(norm_g.shape[0]):
        h = _layer(h, l, norm_g[l], w_in[l], b_gate[l], q_norm_g[l], k_norm_g[l], lambda_q1[l],
                   lambda_k1[l], lambda_q2[l], lambda_k2[l], subln_g[l], w_attn_out[l], conv_w[l],
                   conv_b[l], dt_bias[l], a_log[l], d_skip[l], ssm_norm_g[l], w_ssm_out[l], w_out[l])
    return h
```

```python
import functools
import math

import numpy as np
import jax
import jax.numpy as jnp
from jax import lax
from jax.experimental import pallas as pl
from jax.experimental.pallas import tpu as pltpu

D_MODEL = 1024
H_ATT = 8
HD = 64
HEAD_W = 2 * HD
ATT_W = H_ATT * HEAD_W
D_INNER = 2048
SSM_HEADDIM = 64
H_SSM = D_INNER // SSM_HEADDIM
N_GROUPS = 4
HEADS_PER_GROUP = H_SSM // N_GROUPS
D_STATE = 128
D_CONV = 4
CONV_DIM = D_INNER + 2 * N_GROUPS * D_STATE
CHUNK = 128
EPS = 1e-5
LANES = 128
VMEM_LIMIT = 48 * 1024 * 1024

COL_Q, COL_K, COL_V, COL_G = 0, ATT_W, 2 * ATT_W, 3 * ATT_W
COL_Z = 4 * ATT_W
COL_XBC = COL_Z + D_INNER
COL_GATE = COL_XBC + CONV_DIM
N_MAIN = COL_GATE + 2 * D_MODEL

LOG2E = math.log2(math.e)
SAFE_SCORE_BOUND = 100.0

_F32 = jnp.float32
_BF16 = jnp.bfloat16
_NEG = -1e30


def _dot(a, b):
    return jnp.dot(a, b, preferred_element_type=_F32)


def _sigmoid(v):
    return 0.5 * jnp.tanh(0.5 * v) + 0.5


def _silu(v):
    h = 0.5 * v
    return h * jnp.tanh(h) + h


def _in_proj_kernel(x_ref, g_ref, w_ref, wdt_ref, proj_ref, dt_ref, xn_ref):
    @pl.when(pl.program_id(1) == 0)
    def _():
        xf = x_ref[...]
        ms = jnp.mean(xf * xf, axis=-1, keepdims=True)
        xn = (xf * lax.rsqrt(ms + EPS) * g_ref[...]).astype(_BF16)
        xn_ref[...] = xn
        dt_ref[...] = _dot(xn, wdt_ref[...])

    proj_ref[...] = _dot(xn_ref[...], w_ref[...]).astype(_BF16)


def _in_proj(x2, norm_g, w_main, w_dt, tm, tn):
    m = x2.shape[0]
    return pl.pallas_call(
        _in_proj_kernel,
        grid=(m // tm, N_MAIN // tn),
        in_specs=[
            pl.BlockSpec((tm, D_MODEL), lambda i, j: (i, 0)),
            pl.BlockSpec((1, D_MODEL), lambda i, j: (0, 0)),
            pl.BlockSpec((D_MODEL, tn), lambda i, j: (0, j)),
            pl.BlockSpec((D_MODEL, LANES), lambda i, j: (0, 0)),
        ],
        out_specs=[
            pl.BlockSpec((tm, tn), lambda i, j: (i, j)),
            pl.BlockSpec((tm, LANES), lambda i, j: (i, 0)),
        ],
        out_shape=[
            jax.ShapeDtypeStruct((m, N_MAIN), _BF16),
            jax.ShapeDtypeStruct((m, LANES), _F32),
        ],
        scratch_shapes=[pltpu.VMEM((tm, D_MODEL), _BF16)],
        compiler_params=pltpu.CompilerParams(
            dimension_semantics=("parallel", "arbitrary"), vmem_limit_bytes=VMEM_LIMIT),
        name="in_proj",
    )(x2, norm_g, w_main, w_dt)


def _half_rms(v, gain2, lo):
    sq = v * v
    s_all = jnp.sum(sq, axis=-1, keepdims=True)
    s_lo = jnp.sum(jnp.where(lo, sq, 0.0), axis=-1, keepdims=True)
    ms = jnp.where(lo, s_lo, s_all - s_lo) * (1.0 / HD)
    return v * lax.rsqrt(ms + EPS) * gain2


def _bf16_terms(v, n):
    terms = []
    for _ in range(n):
        t = v.astype(_BF16).astype(_F32)
        terms.append(t)
        v = v - t
    return terms


N_FEAT = 8


def _alibi_tables(s_len):
    slopes = 2.0 ** (-8.0 * np.arange(1, H_ATT + 1) / H_ATT)
    slope2 = jnp.asarray(slopes * LOG2E, _F32)
    pos = jnp.arange(s_len, dtype=jnp.int32)
    pos_a = (pos >> 7).astype(_F32)
    pos_b = (pos & 127).astype(_F32)
    over_s = lambda v: jnp.broadcast_to(v[:, None], (H_ATT, s_len))
    q_feats = ([over_s(t) for t in _bf16_terms(slope2 * 128.0, 3) + _bf16_terms(slope2, 3)]
               + _bf16_terms(-slope2[:, None] * pos.astype(_F32)[None, :], 2))
    k_feats = [pos_a] * 3 + [pos_b] * 3 + [jnp.ones((s_len,), _F32)] * 2
    qf = jnp.stack(q_feats, axis=-1)
    kf = jnp.stack(k_feats, axis=-1)
    widen = lambda t: jnp.tile(jnp.pad(t, [(0, 0)] * (t.ndim - 1) + [(0, HD - N_FEAT)]), 2).astype(_BF16)
    return widen(qf), widen(kf)


def _attn_prep_kernel(q_ref, k_ref, v_ref, gq_ref, gk_ref, qtab_ref, ktab_ref,
                      qf_ref, kf_ref, vT_ref, *, ts):
    lane = lax.broadcasted_iota(jnp.int32, (ts, HEAD_W), 1)
    lo = lane < HD
    for h in range(H_ATT):
        hs = slice(h * HEAD_W, (h + 1) * HEAD_W)
        qn = _half_rms(q_ref[:, hs].astype(_F32), gq_ref[...], lo) * (HD ** -0.5 * LOG2E)
        kn = _half_rms(k_ref[:, hs].astype(_F32), gk_ref[...], lo)
        qn_b, kn_b = qn.astype(_BF16), kn.astype(_BF16)
        for c in range(2):
            mine = lo if c == 0 else jnp.logical_not(lo)
            qf_ref[h, c] = jnp.where(mine, qn_b, qtab_ref[h])
            kf_ref[h, c] = jnp.where(mine, kn_b, ktab_ref[...])
        vT_ref[h] = v_ref[:, hs].astype(_F32).T.astype(_BF16)


def _attn_prep(proj, gq2, gk2, qtab, ktab, bsz, s_len, ts):
    ns = s_len // ts
    blk = lambda col0: pl.BlockSpec((ts, ATT_W), lambda b, i: (b * ns + i, col0 // ATT_W))
    vec = pl.BlockSpec((1, HEAD_W), lambda b, i: (0, 0))
    return pl.pallas_call(
        functools.partial(_attn_prep_kernel, ts=ts),
        grid=(bsz, ns),
        in_specs=[blk(COL_Q), blk(COL_K), blk(COL_V), vec, vec,
                  pl.BlockSpec((H_ATT, ts, HEAD_W), lambda b, i: (0, i, 0)),
                  pl.BlockSpec((ts, HEAD_W), lambda b, i: (i, 0))],
        out_specs=[
            pl.BlockSpec((None, H_ATT, 2, ts, HEAD_W), lambda b, i: (b, 0, 0, i, 0)),
            pl.BlockSpec((None, H_ATT, None, 2, ts, HEAD_W), lambda b, i: (b, 0, i, 0, 0, 0)),
            pl.BlockSpec((None, H_ATT, None, HEAD_W, ts), lambda b, i: (b, 0, i, 0, 0)),
        ],
        out_shape=[
            jax.ShapeDtypeStruct((bsz, H_ATT, 2, s_len, HEAD_W), _BF16),
            jax.ShapeDtypeStruct((bsz, H_ATT, ns, 2, ts, HEAD_W), _BF16),
            jax.ShapeDtypeStruct((bsz, H_ATT, ns, HEAD_W, ts), _BF16),
        ],
        compiler_params=pltpu.CompilerParams(
            dimension_semantics=("parallel", "parallel"), vmem_limit_bytes=VMEM_LIMIT),
        name="attn_prep",
    )(proj, proj, proj, gq2, gk2, qtab, ktab)


def _attn_kernel(q_ref, kf_ref, vT_ref, g_ref, sub_ref, lq1_ref, lk1_ref, lq2_ref, lk2_ref,
                 o_ref, m_ref, l_ref, acc_ref, s_ref, *, blk, lam_init, online):
    u = pl.program_id(2)
    m_ref[...] = jnp.full(m_ref.shape, _NEG, _F32)
    l_ref[...] = jnp.zeros(l_ref.shape, _F32)
    acc_ref[...] = jnp.zeros(acc_ref.shape, _F32)
    nt = (((1,), (1,)), ((), ()))

    def scores(j, slot, c0):
        for c in range(2):
            s_ref[slot, c, :, c0:] = lax.dot_general(kf_ref[j, c], q_ref[c, c0:, :], nt,
                                                     preferred_element_type=_F32)

    def consume(j, slot, c0, masked):
        vblk = vT_ref[j]
        for c in range(2):
            s = s_ref[slot, c, :, c0:]
            if masked:
                kpos = lax.broadcasted_iota(jnp.int32, s.shape, 0)
                qpos = lax.broadcasted_iota(jnp.int32, s.shape, 1)
                s = jnp.where(kpos <= qpos, s, _NEG)
            if online:
                m_old = m_ref[c, :, c0:]
                m_new = jnp.maximum(m_old, jnp.max(s, axis=0, keepdims=True))
                alpha = jnp.exp2(m_old - m_new)
                p = jnp.exp2(s - m_new)
                l_ref[c, :, c0:] = alpha * l_ref[c, :, c0:] + jnp.sum(p, axis=0, keepdims=True)
                acc_ref[c, :, c0:] = alpha * acc_ref[c, :, c0:] + _dot(vblk, p.astype(_BF16))
                m_ref[c, :, c0:] = m_new
            else:
                p = jnp.exp2(s)
                l_ref[c, :, c0:] += jnp.sum(p, axis=0, keepdims=True)
                acc_ref[c, :, c0:] += _dot(vblk, p.astype(_BF16))

    scores(0, 0, 0)

    def body(t, carry):
        scores(2 * t + 1, 1, 0)
        consume(2 * t, 0, 0, False)
        scores(2 * t + 2, 0, 0)
        consume(2 * t + 1, 1, 0, False)
        return carry

    lax.fori_loop(0, u, body, 0)
    scores(2 * u + 1, 1, blk)
    consume(2 * u, 0, 0, True)
    consume(2 * u + 1, 1, blk, True)

    lam = (jnp.exp(jnp.sum(lq1_ref[...] * lk1_ref[...], axis=-1, keepdims=True))
           - jnp.exp(jnp.sum(lq2_ref[...] * lk2_ref[...], axis=-1, keepdims=True)) + lam_init)
    oT = acc_ref[0] / l_ref[0] - lam * (acc_ref[1] / l_ref[1])
    o = oT.T
    o = o * lax.rsqrt(jnp.mean(o * o, axis=-1, keepdims=True) + EPS) * sub_ref[...] * (1.0 - lam_init)
    o_ref[...] = (o * _silu(g_ref[...].astype(_F32))).astype(_BF16)


def _attention(qf, kf, vT, proj, subln_g, lq1, lk1, lq2, lk2, *, bsz, s_len, blk, lam_init, online):
    nk = s_len // blk
    tq = 2 * blk
    nq = s_len // tq
    vec = lambda w: pl.BlockSpec((1, w), lambda b, h, i: (0, 0))
    return pl.pallas_call(
        functools.partial(_attn_kernel, blk=blk, lam_init=lam_init, online=online),
        grid=(bsz, H_ATT, nq),
        in_specs=[
            pl.BlockSpec((None, None, 2, tq, HEAD_W), lambda b, h, i: (b, h, 0, i, 0)),
            pl.BlockSpec((None, None, nk, 2, blk, HEAD_W), lambda b, h, i: (b, h, 0, 0, 0, 0)),
            pl.BlockSpec((None, None, nk, HEAD_W, blk), lambda b, h, i: (b, h, 0, 0, 0)),
            pl.BlockSpec((tq, HEAD_W), lambda b, h, i: (b * nq + i, COL_G // HEAD_W + h)),
            vec(HEAD_W), vec(HD), vec(HD), vec(HD), vec(HD),
        ],
        out_specs=pl.BlockSpec((tq, HEAD_W), lambda b, h, i: (b * nq + i, h)),
        out_shape=jax.ShapeDtypeStruct((bsz * s_len, ATT_W), _BF16),
        scratch_shapes=[
            pltpu.VMEM((2, 1, tq), _F32),
            pltpu.VMEM((2, 1, tq), _F32),
            pltpu.VMEM((2, HEAD_W, tq), _F32),
            pltpu.VMEM((2, 2, blk, tq), _F32),
        ],
        compiler_params=pltpu.CompilerParams(
            dimension_semantics=("parallel", "parallel", "arbitrary"), vmem_limit_bytes=VMEM_LIMIT),
        name="diff_attention",
    )(qf, kf, vT, proj, subln_g, lq1, lk1, lq2, lk2)


def _split3(v):
    hi = v.astype(_BF16)
    r1 = v - hi.astype(_F32)
    mid = r1.astype(_BF16)
    lo = (r1 - mid.astype(_F32)).astype(_BF16)
    return hi, mid, lo


CONV_HIST = 16
CONV_TILE = 512


def _shift_matrix():
    sm = np.zeros(((D_CONV - 1) * CHUNK, CONV_HIST + CHUNK), np.float32)
    for j in range(D_CONV - 1):
        t = np.arange(CHUNK)
        sm[j * CHUNK + t, CONV_HIST + t - (D_CONV - 1) + j] = 1.0
    return jnp.asarray(sm, _BF16)


def _head_expand_matrix():
    em = np.zeros((2 * LANES, D_INNER), np.float32)
    cols = np.arange(D_INNER)
    em[cols // SSM_HEADDIM, cols] = 1.0
    em[LANES + cols // SSM_HEADDIM, cols] = 1.0
    return jnp.asarray(em, _BF16)


def _ssd_kernel(xbc_ref, z_ref, dt_ref, cw_ref, cb_ref, dtb_ref, alog_ref, dsk_ref, ng_ref,
                shift_ref, expand_ref, y_ref, state_ref, ext_ref, act_ref, spread_ref, yacc_ref):
    L = CHUNK

    @pl.when(pl.program_id(1) == 0)
    def _():
        state_ref[...] = jnp.zeros(state_ref.shape, _F32)
        ext_ref[0:CONV_HIST, :] = jnp.zeros((CONV_HIST, CONV_DIM), _BF16)

    @pl.when(pl.program_id(1) > 0)
    def _():
        ext_ref[0:CONV_HIST, :] = ext_ref[L:L + CONV_HIST, :]

    ext_ref[CONV_HIST:, :] = xbc_ref[...]

    for ct in range(CONV_DIM // CONV_TILE):
        cs = slice(ct * CONV_TILE, (ct + 1) * CONV_TILE)
        shifted = _dot(shift_ref[...], ext_ref[:, cs])
        conv = cb_ref[:, cs] + cw_ref[D_CONV - 1:D_CONV, cs] * ext_ref[CONV_HIST:, cs].astype(_F32)
        for t in range(D_CONV - 1):
            conv = conv + cw_ref[t:t + 1, cs] * shifted[t * L:(t + 1) * L, :]
        act_ref[:, cs] = _silu(conv)

    dt = jax.nn.softplus(dt_ref[...] + dtb_ref[...])
    a = dt * (-jnp.exp(alog_ref[...]))
    ri = lax.broadcasted_iota(jnp.int32, (L, L), 0)
    ci = lax.broadcasted_iota(jnp.int32, (L, L), 1)
    tril = ri >= ci
    tri_b = tril.astype(_BF16)
    a_hi, a_mid, a_lo = _split3(a)
    acum = _dot(tri_b, a_hi) + _dot(tri_b, a_mid) + _dot(tri_b, a_lo)
    acum_t = acum.T
    a_last = acum[L - 1:L, :]
    eac = jnp.exp(acum)
    decay = jnp.exp(a_last - acum)

    stack = jnp.concatenate([dt, eac, decay], axis=0)
    st_hi = stack.astype(_BF16)
    st_lo = (stack - st_hi.astype(_F32)).astype(_BF16)
    spread_ref[...] = _dot(jnp.concatenate([st_hi, st_lo], axis=1), expand_ref[...])
    dt_x = lambda cs: spread_ref[0:L, cs]
    eac_x = lambda cs: spread_ref[L:2 * L, cs]
    decay_x = lambda cs: spread_ref[2 * L:3 * L, cs]
    chunk_decay_x = lambda cs: spread_ref[2 * L - 1:2 * L, cs]

    lane = lax.broadcasted_iota(jnp.int32, (L, LANES), 1)
    lo = lane < SSM_HEADDIM

    for g in range(N_GROUPS):
        c_b = D_INNER + g * D_STATE
        c_c = D_INNER + N_GROUPS * D_STATE + g * D_STATE
        bm_b = act_ref[:, c_b:c_b + D_STATE].astype(_BF16)
        cm_b = act_ref[:, c_c:c_c + D_STATE].astype(_BF16)
        cb = lax.dot_general(cm_b, bm_b, (((1,), (1,)), ((), ())), preferred_element_type=_F32)
        gw = HEADS_PER_GROUP * SSM_HEADDIM
        gcol = g * gw
        y_off = _dot(cm_b, state_ref[:, gcol:gcol + gw].astype(_BF16))
        xdd_parts = []
        for jj in range(HEADS_PER_GROUP // 2):
            j = g * (HEADS_PER_GROUP // 2) + jj
            col = j * LANES
            cs = slice(col, col + LANES)
            xblk = act_ref[:, cs]
            xdt = xblk * dt_x(cs)
            ms = []
            for hh in range(2):
                h = 2 * j + hh
                diff = acum[:, h:h + 1] - acum_t[h:h + 1, :]
                lmat = jnp.exp(jnp.where(tril, diff, _NEG))
                ms.append((cb * lmat).astype(_BF16))
            lhs = jnp.concatenate(ms, axis=1)
            xdt_b = xdt.astype(_BF16)
            zero = jnp.zeros_like(xdt_b)
            rhs = jnp.concatenate([jnp.where(lo, xdt_b, zero), jnp.where(lo, zero, xdt_b)], axis=0)
            y = _dot(lhs, rhs)
            y = y + y_off[:, jj * LANES:(jj + 1) * LANES] * eac_x(cs)
            y = y + xblk * dsk_ref[:, cs]
            yacc_ref[:, cs] = y
            xdd_parts.append((xdt * decay_x(cs)).astype(_BF16))
        xdd = jnp.concatenate(xdd_parts, axis=1)
        upd = lax.dot_general(bm_b, xdd, (((0,), (0,)), ((), ())), preferred_element_type=_F32)
        gs = slice(gcol, gcol + gw)
        state_ref[:, gs] = state_ref[:, gs] * chunk_decay_x(gs) + upd

    for g in range(N_GROUPS):
        gw = D_INNER // N_GROUPS
        sl = slice(g * gw, (g + 1) * gw)
        yg = yacc_ref[:, sl] * _silu(z_ref[:, sl].astype(_F32))
        yg = yg * lax.rsqrt(jnp.mean(yg * yg, axis=-1, keepdims=True) + EPS) * ng_ref[:, sl]
        y_ref[:, sl] = yg.astype(y_ref.dtype)


def _ssd(proj, dt_raw, conv_w, conv_b, dt_bias, a_log, d_skip_x, norm_g, bsz, s_len):
    nc = s_len // CHUNK
    m = bsz * s_len
    full = lambda r, w: pl.BlockSpec((r, w), lambda b, c: (0, 0))
    return pl.pallas_call(
        _ssd_kernel,
        grid=(bsz, nc),
        in_specs=[
            pl.BlockSpec((CHUNK, CONV_DIM), lambda b, c: (b * nc + c, COL_XBC // CONV_DIM)),
            pl.BlockSpec((CHUNK, D_INNER), lambda b, c: (b * nc + c, COL_Z // D_INNER)),
            pl.BlockSpec((CHUNK, LANES), lambda b, c: (b * nc + c, 0)),
            full(D_CONV, CONV_DIM), full(1, CONV_DIM), full(1, LANES), full(1, LANES),
            full(1, D_INNER), full(1, D_INNER),
            full((D_CONV - 1) * CHUNK, CONV_HIST + CHUNK), full(2 * LANES, D_INNER),
        ],
        out_specs=pl.BlockSpec((CHUNK, D_INNER), lambda b, c: (b * nc + c, 0)),
        out_shape=jax.ShapeDtypeStruct((m, D_INNER), _BF16),
        scratch_shapes=[
            pltpu.VMEM((D_STATE, D_INNER), _F32),
            pltpu.VMEM((CONV_HIST + CHUNK, CONV_DIM), _BF16),
            pltpu.VMEM((CHUNK, CONV_DIM), _F32),
            pltpu.VMEM((3 * CHUNK, D_INNER), _F32),
            pltpu.VMEM((CHUNK, D_INNER), _F32),
        ],
        compiler_params=pltpu.CompilerParams(
            dimension_semantics=("parallel", "arbitrary"), vmem_limit_bytes=VMEM_LIMIT),
        name="ssd",
    )(proj, proj, dt_raw, conv_w, conv_b, dt_bias, a_log, d_skip_x, norm_g,
      _shift_matrix(), _head_expand_matrix())


def _merge_kernel(x_ref, oa_ref, ys_ref, gla_ref, gls_ref, bg_ref, wao_ref, wso_ref, wo_ref, out_ref):
    y_att = _dot(oa_ref[...], wao_ref[...])
    y_ssm = _dot(ys_ref[...], wso_ref[...])
    g_a = _sigmoid(gla_ref[...].astype(_F32) + bg_ref[:, 0:D_MODEL])
    g_s = _sigmoid(gls_ref[...].astype(_F32) + bg_ref[:, D_MODEL:2 * D_MODEL])
    h = (g_a * y_att + g_s * y_ssm).astype(_BF16)
    out_ref[...] = x_ref[...] + _dot(h, wo_ref[...])


def _merge(x2, o_att, y_ssm, proj, b_gate, w_ao, w_so, w_o, tm):
    m = x2.shape[0]
    row = lambda w, cb=0: pl.BlockSpec((tm, w), lambda i: (i, cb))
    full = lambda r, w: pl.BlockSpec((r, w), lambda i: (0, 0))
    return pl.pallas_call(
        _merge_kernel,
        grid=(m // tm,),
        in_specs=[
            row(D_MODEL), row(ATT_W), row(D_INNER),
            row(D_MODEL, COL_GATE // D_MODEL), row(D_MODEL, COL_GATE // D_MODEL + 1),
            full(1, 2 * D_MODEL), full(ATT_W, D_MODEL), full(D_INNER, D_MODEL), full(D_MODEL, D_MODEL),
        ],
        out_specs=row(D_MODEL),
        out_shape=jax.ShapeDtypeStruct((m, D_MODEL), x2.dtype),
        compiler_params=pltpu.CompilerParams(
            dimension_semantics=("parallel",), vmem_limit_bytes=VMEM_LIMIT),
        name="merge",
    )(x2, o_att, y_ssm, proj, proj, b_gate, w_ao, w_so, w_o)


def _layer(x, layer_idx, norm_g, w_in, b_gate, q_norm_g, k_norm_g, lq1, lk1, lq2, lk2, subln_g,
           w_attn_out, conv_w, conv_b, dt_bias, a_log, d_skip, ssm_norm_g, w_ssm_out, w_out):
    bsz, s_len, _ = x.shape
    m = bsz * s_len
    x2 = x.reshape(m, D_MODEL)
    lam_init = 0.8 - 0.6 * math.exp(-0.3 * layer_idx)

    c_dt = COL_GATE
    w_main = jnp.concatenate([w_in[:, :c_dt], w_in[:, c_dt + H_SSM:]], axis=1).astype(_BF16)
    w_dt = jnp.pad(w_in[:, c_dt:c_dt + H_SSM], ((0, 0), (0, LANES - H_SSM))).astype(_BF16)
    pad_h = lambda v: jnp.pad(v.astype(_F32), (0, LANES - H_SSM)).reshape(1, LANES)

    tm = min(2048, m)
    proj, dt_raw = _in_proj(x2, norm_g.reshape(1, D_MODEL), w_main, w_dt, tm, 1024)

    blk = min(512, s_len // 2)
    qtab, ktab = _alibi_tables(s_len)
    two = lambda v: jnp.tile(v.astype(_F32), 2).reshape(1, HEAD_W)
    qf, kf, vT = _attn_prep(proj, two(q_norm_g), two(k_norm_g), qtab, ktab, bsz, s_len, blk)
    r64 = lambda v: v.astype(_F32).reshape(1, HD)
    attn_args = (qf, kf, vT, proj, subln_g.reshape(1, HEAD_W).astype(_F32),
                 r64(lq1), r64(lk1), r64(lq2), r64(lk2))
    attn = functools.partial(_attention, bsz=bsz, s_len=s_len, blk=blk, lam_init=lam_init)
    bound = (HD ** 0.5 * LOG2E) * jnp.max(jnp.abs(q_norm_g)) * jnp.max(jnp.abs(k_norm_g))
    o_att = lax.cond(bound < SAFE_SCORE_BOUND,
                     functools.partial(attn, online=False), functools.partial(attn, online=True),
                     *attn_args)

    y_ssm = _ssd(proj, dt_raw, conv_w.astype(_F32), conv_b.reshape(1, CONV_DIM).astype(_F32),
                 pad_h(dt_bias), pad_h(a_log),
                 jnp.repeat(d_skip.astype(_F32), SSM_HEADDIM).reshape(1, D_INNER),
                 ssm_norm_g.reshape(1, D_INNER).astype(_F32), bsz, s_len)

    out = _merge(x2, o_att, y_ssm, proj, b_gate.reshape(1, 2 * D_MODEL).astype(_F32),
                 w_attn_out.astype(_BF16), w_ssm_out.astype(_BF16), w_out.astype(_BF16), min(512, m))
    return out.reshape(bsz, s_len, D_MODEL)


def kernel(x, norm_g, w_in, b_gate, q_norm_g, k_norm_g, lambda_q1, lambda_k1, lambda_q2, lambda_k2,
           subln_g, w_attn_out, conv_w, conv_b, dt_bias, a_log, d_skip, ssm_norm_g, w_ssm_out, w_out):
    h = x
    for l in range(norm_g.shape[0]):
        h = _layer(h, l, norm_g[l], w_in[l], b_gate[l], q_norm_g[l], k_norm_g[l], lambda_q1[l],
                   lambda_k1[l], lambda_q2[l], lambda_k2[l], subln_g[l], w_attn_out[l], conv_w[l],
                   conv_b[l], dt_bias[l], a_log[l], d_skip[l], ssm_norm_g[l], w_ssm_out[l], w_out[l])
    return h
```

```python
import functools
import math

import numpy as np
import jax
import jax.numpy as jnp
from jax import lax
from jax.experimental import pallas as pl
from jax.experimental.pallas import tpu as pltpu

D_MODEL = 1024
H_ATT = 8
HD = 64
HEAD_W = 2 * HD
ATT_W = H_ATT * HEAD_W
D_INNER = 2048
SSM_HEADDIM = 64
H_SSM = D_INNER // SSM_HEADDIM
N_GROUPS = 4
HEADS_PER_GROUP = H_SSM // N_GROUPS
D_STATE = 128
D_CONV = 4
CONV_DIM = D_INNER + 2 * N_GROUPS * D_STATE
CHUNK = 128
EPS = 1e-5
LANES = 128
VMEM_LIMIT = 48 * 1024 * 1024

COL_Q, COL_K, COL_V, COL_G = 0, ATT_W, 2 * ATT_W, 3 * ATT_W
COL_Z = 4 * ATT_W
COL_XBC = COL_Z + D_INNER
COL_GATE = COL_XBC + CONV_DIM
N_MAIN = COL_GATE + 2 * D_MODEL

LOG2E = math.log2(math.e)
SAFE_SCORE_BOUND = 100.0

_F32 = jnp.float32
_BF16 = jnp.bfloat16
_NEG = -1e30


def _dot(a, b):
    return jnp.dot(a, b, preferred_element_type=_F32)


def _sigmoid(v):
    return 0.5 * jnp.tanh(0.5 * v) + 0.5


def _silu_of_half(h):
    return h * jnp.tanh(h) + h


def _silu(v):
    return _silu_of_half(0.5 * v)


def _in_proj_kernel(x_ref, g_ref, w_ref, wdt_ref, proj_ref, dt_ref, xn_ref):
    @pl.when(pl.program_id(1) == 0)
    def _():
        xf = x_ref[...]
        ms = jnp.mean(xf * xf, axis=-1, keepdims=True)
        xn = (xf * lax.rsqrt(ms + EPS) * g_ref[...]).astype(_BF16)
        xn_ref[...] = xn
        dt_ref[...] = _dot(xn, wdt_ref[...])

    proj_ref[...] = _dot(xn_ref[...], w_ref[...]).astype(_BF16)


def _in_proj(x2, norm_g, w_main, w_dt, tm, tn):
    m = x2.shape[0]
    return pl.pallas_call(
        _in_proj_kernel,
        grid=(m // tm, N_MAIN // tn),
        in_specs=[
            pl.BlockSpec((tm, D_MODEL), lambda i, j: (i, 0)),
            pl.BlockSpec((1, D_MODEL), lambda i, j: (0, 0)),
            pl.BlockSpec((D_MODEL, tn), lambda i, j: (0, j)),
            pl.BlockSpec((D_MODEL, LANES), lambda i, j: (0, 0)),
        ],
        out_specs=[
            pl.BlockSpec((tm, tn), lambda i, j: (i, j)),
            pl.BlockSpec((tm, LANES), lambda i, j: (i, 0)),
        ],
        out_shape=[
            jax.ShapeDtypeStruct((m, N_MAIN), _BF16),
            jax.ShapeDtypeStruct((m, LANES), _F32),
        ],
        scratch_shapes=[pltpu.VMEM((tm, D_MODEL), _BF16)],
        compiler_params=pltpu.CompilerParams(
            dimension_semantics=("parallel", "arbitrary"), vmem_limit_bytes=VMEM_LIMIT),
        name="in_proj",
    )(x2, norm_g, w_main, w_dt)


def _half_rms(v, gain2, lo):
    sq = v * v
    s_all = jnp.sum(sq, axis=-1, keepdims=True)
    s_lo = jnp.sum(jnp.where(lo, sq, 0.0), axis=-1, keepdims=True)
    ms = jnp.where(lo, s_lo, s_all - s_lo) * (1.0 / HD)
    return v * lax.rsqrt(ms + EPS) * gain2


def _bf16_terms(v, n):
    terms = []
    for _ in range(n):
        t = v.astype(_BF16).astype(_F32)
        terms.append(t)
        v = v - t
    return terms


N_FEAT = 8


def _alibi_tables(s_len):
    slopes = 2.0 ** (-8.0 * np.arange(1, H_ATT + 1) / H_ATT)
    slope2 = jnp.asarray(slopes * LOG2E, _F32)
    pos = jnp.arange(s_len, dtype=jnp.int32)
    pos_a = (pos >> 7).astype(_F32)
    pos_b = (pos & 127).astype(_F32)
    over_s = lambda v: jnp.broadcast_to(v[:, None], (H_ATT, s_len))
    q_feats = ([over_s(t) for t in _bf16_terms(slope2 * 128.0, 3) + _bf16_terms(slope2, 3)]
               + _bf16_terms(-slope2[:, None] * pos.astype(_F32)[None, :], 2))
    k_feats = [pos_a] * 3 + [pos_b] * 3 + [jnp.ones((s_len,), _F32)] * 2
    qf = jnp.stack(q_feats, axis=-1)
    kf = jnp.stack(k_feats, axis=-1)
    widen = lambda t: jnp.tile(jnp.pad(t, [(0, 0)] * (t.ndim - 1) + [(0, HD - N_FEAT)]), 2).astype(_BF16)
    return widen(qf), widen(kf)


def _attn_prep_kernel(q_ref, k_ref, v_ref, gq_ref, gk_ref, qtab_ref, ktab_ref,
                      qf_ref, kf_ref, vT_ref, *, ts):
    lane = lax.broadcasted_iota(jnp.int32, (ts, HEAD_W), 1)
    lo = lane < HD
    for h in range(H_ATT):
        hs = slice(h * HEAD_W, (h + 1) * HEAD_W)
        qn = _half_rms(q_ref[:, hs].astype(_F32), gq_ref[...], lo) * (HD ** -0.5 * LOG2E)
        kn = _half_rms(k_ref[:, hs].astype(_F32), gk_ref[...], lo)
        qn_b, kn_b = qn.astype(_BF16), kn.astype(_BF16)
        for c in range(2):
            mine = lo if c == 0 else jnp.logical_not(lo)
            qf_ref[h, c] = jnp.where(mine, qn_b, qtab_ref[h])
            kf_ref[h, c] = jnp.where(mine, kn_b, ktab_ref[...])
        vT_ref[h] = v_ref[:, hs].astype(_F32).T.astype(_BF16)


def _attn_prep(proj, gq2, gk2, qtab, ktab, bsz, s_len, ts):
    ns = s_len // ts
    blk = lambda col0: pl.BlockSpec((ts, ATT_W), lambda b, i: (b * ns + i, col0 // ATT_W))
    vec = pl.BlockSpec((1, HEAD_W), lambda b, i: (0, 0))
    return pl.pallas_call(
        functools.partial(_attn_prep_kernel, ts=ts),
        grid=(bsz, ns),
        in_specs=[blk(COL_Q), blk(COL_K), blk(COL_V), vec, vec,
                  pl.BlockSpec((H_ATT, ts, HEAD_W), lambda b, i: (0, i, 0)),
                  pl.BlockSpec((ts, HEAD_W), lambda b, i: (i, 0))],
        out_specs=[
            pl.BlockSpec((None, H_ATT, 2, ts, HEAD_W), lambda b, i: (b, 0, 0, i, 0)),
            pl.BlockSpec((None, H_ATT, None, 2, ts, HEAD_W), lambda b, i: (b, 0, i, 0, 0, 0)),
            pl.BlockSpec((None, H_ATT, None, HEAD_W, ts), lambda b, i: (b, 0, i, 0, 0)),
        ],
        out_shape=[
            jax.ShapeDtypeStruct((bsz, H_ATT, 2, s_len, HEAD_W), _BF16),
            jax.ShapeDtypeStruct((bsz, H_ATT, ns, 2, ts, HEAD_W), _BF16),
            jax.ShapeDtypeStruct((bsz, H_ATT, ns, HEAD_W, ts), _BF16),
        ],
        compiler_params=pltpu.CompilerParams(
            dimension_semantics=("parallel", "parallel"), vmem_limit_bytes=VMEM_LIMIT),
        name="attn_prep",
    )(proj, proj, proj, gq2, gk2, qtab, ktab)


def _attn_kernel(q_ref, kf_ref, vT_ref, g_ref, sub_ref, lq1_ref, lk1_ref, lq2_ref, lk2_ref,
                 o_ref, m_ref, l_ref, acc_ref, s_ref, *, blk, nq, lam_init, online):
    u = pl.program_id(2)
    tq = 2 * blk
    m_ref[...] = jnp.full(m_ref.shape, _NEG, _F32)
    l_ref[...] = jnp.zeros(l_ref.shape, _F32)
    acc_ref[...] = jnp.zeros(acc_ref.shape, _F32)
    nt = (((1,), (1,)), ((), ()))

    def scores(tile, j, slot, c0):
        row0 = pl.multiple_of(tile * tq + c0, blk)
        for c in range(2):
            s_ref[slot, c, :, c0:] = lax.dot_general(kf_ref[j, c], q_ref[c, pl.ds(row0, tq - c0), :], nt,
                                                     preferred_element_type=_F32)

    def masked_scores(slot, c, c0, masked):
        s = s_ref[slot, c, :, c0:]
        if masked:
            kpos = lax.broadcasted_iota(jnp.int32, s.shape, 0)
            qpos = lax.broadcasted_iota(jnp.int32, s.shape, 1)
            s = jnp.where(kpos <= qpos, s, _NEG)
        return s

    def consume(j, slot, c0, masked):
        vblk = vT_ref[j]
        for c in range(2):
            s = masked_scores(slot, c, c0, masked)
            if online:
                m_old = m_ref[c, :, c0:]
                m_new = jnp.maximum(m_old, jnp.max(s, axis=0, keepdims=True))
                alpha = jnp.exp2(m_old - m_new)
                p = jnp.exp2(s - m_new)
                l_ref[c, :, c0:] = alpha * l_ref[c, :, c0:] + jnp.sum(p, axis=0, keepdims=True)
                acc_ref[c, :, c0:] = alpha * acc_ref[c, :, c0:] + _dot(vblk, p.astype(_BF16))
                m_ref[c, :, c0:] = m_new
            else:
                p = jnp.exp2(s)
                l_ref[c, :, c0:] += jnp.sum(p, axis=0, keepdims=True)
                acc_ref[c, :, c0:] += _dot(vblk, p.astype(_BF16))

    @pl.when(u == 0)
    def _():
        scores(0, 0, 0, 0)

    def body(t, carry):
        scores(u, 2 * t + 1, 1, 0)
        consume(2 * t, 0, 0, False)
        scores(u, 2 * t + 2, 0, 0)
        consume(2 * t + 1, 1, 0, False)
        return carry

    lax.fori_loop(0, u, body, 0)
    scores(u, 2 * u + 1, 1, blk)
    consume(2 * u, 0, 0, True)
    scores(jnp.minimum(u + 1, nq - 1), 0, 0, 0)
    consume(2 * u + 1, 1, blk, True)

    lam = (jnp.exp(jnp.sum(lq1_ref[...] * lk1_ref[...], axis=-1, keepdims=True))
           - jnp.exp(jnp.sum(lq2_ref[...] * lk2_ref[...], axis=-1, keepdims=True)) + lam_init)
    oT = acc_ref[0] / l_ref[0] - lam * (acc_ref[1] / l_ref[1])
    o = oT.T
    o = o * lax.rsqrt(jnp.mean(o * o, axis=-1, keepdims=True) + EPS) * sub_ref[...] * (1.0 - lam_init)
    o_ref[...] = (o * _silu(g_ref[...].astype(_F32))).astype(_BF16)


def _attention(qf, kf, vT, proj, subln_g, lq1, lk1, lq2, lk2, *, bsz, s_len, blk, lam_init, online):
    nk = s_len // blk
    tq = 2 * blk
    nq = s_len // tq
    vec = lambda w: pl.BlockSpec((1, w), lambda b, h, i: (0, 0))
    return pl.pallas_call(
        functools.partial(_attn_kernel, blk=blk, nq=nq, lam_init=lam_init, online=online),
        grid=(bsz, H_ATT, nq),
        in_specs=[
            pl.BlockSpec((None, None, 2, s_len, HEAD_W), lambda b, h, i: (b, h, 0, 0, 0)),
            pl.BlockSpec((None, None, nk, 2, blk, HEAD_W), lambda b, h, i: (b, h, 0, 0, 0, 0)),
            pl.BlockSpec((None, None, nk, HEAD_W, blk), lambda b, h, i: (b, h, 0, 0, 0)),
            pl.BlockSpec((tq, HEAD_W), lambda b, h, i: (b * nq + i, COL_G // HEAD_W + h)),
            vec(HEAD_W), vec(HD), vec(HD), vec(HD), vec(HD),
        ],
        out_specs=pl.BlockSpec((tq, HEAD_W), lambda b, h, i: (b * nq + i, h)),
        out_shape=jax.ShapeDtypeStruct((bsz * s_len, ATT_W), _BF16),
        scratch_shapes=[
            pltpu.VMEM((2, 1, tq), _F32),
            pltpu.VMEM((2, 1, tq), _F32),
            pltpu.VMEM((2, HEAD_W, tq), _F32),
            pltpu.VMEM((2, 2, blk, tq), _F32),
        ],
        compiler_params=pltpu.CompilerParams(
            dimension_semantics=("parallel", "parallel", "arbitrary"), vmem_limit_bytes=VMEM_LIMIT),
        name="diff_attention",
    )(qf, kf, vT, proj, subln_g, lq1, lk1, lq2, lk2)


def _split3(v):
    hi = v.astype(_BF16)
    r1 = v - hi.astype(_F32)
    mid = r1.astype(_BF16)
    lo = (r1 - mid.astype(_F32)).astype(_BF16)
    return hi, mid, lo


CONV_HIST = 16
CONV_TILE = 512


CONV_ROWS = CONV_HIST + CHUNK


def _shift_matrix():
    sm = np.zeros((CHUNK, (D_CONV - 1) * CONV_ROWS), np.float32)
    t = np.arange(CHUNK)
    for j in range(D_CONV - 1):
        sm[t, j * CONV_ROWS + CONV_HIST + t - (D_CONV - 1) + j] = 1.0
    return jnp.asarray(sm, _BF16)


def _head_expand_matrix():
    em = np.zeros((2 * LANES, D_INNER), np.float32)
    cols = np.arange(D_INNER)
    em[cols // SSM_HEADDIM, cols] = 1.0
    em[LANES + cols // SSM_HEADDIM, cols] = 1.0
    return jnp.asarray(em, _BF16)


def _ssd_kernel(xbc_ref, z_ref, dt_ref, cw_ref, cwx_ref, cb_ref, dtb_ref, alog_ref, dsk_ref, ng_ref,
                shift_ref, expand_ref, y_ref, state_ref, ext_ref, act_ref, spread_ref, yacc_ref):
    L = CHUNK

    @pl.when(pl.program_id(1) == 0)
    def _():
        state_ref[...] = jnp.zeros(state_ref.shape, _F32)
        ext_ref[0:CONV_HIST, :] = jnp.zeros((CONV_HIST, CONV_DIM), _BF16)

    @pl.when(pl.program_id(1) > 0)
    def _():
        ext_ref[0:CONV_HIST, :] = ext_ref[L:L + CONV_HIST, :]

    ext_ref[CONV_HIST:, :] = xbc_ref[...]

    for ct in range(CONV_DIM // CONV_TILE):
        cs = slice(ct * CONV_TILE, (ct + 1) * CONV_TILE)
        ext = ext_ref[:, cs]
        taps = jnp.concatenate([ext * cwx_ref[t, :, cs] for t in range(D_CONV - 1)], axis=0)
        now = cb_ref[:, cs] + cw_ref[D_CONV - 1:D_CONV, cs] * ext_ref[CONV_HIST:, cs].astype(_F32)
        act_ref[:, cs] = _silu_of_half(now + _dot(shift_ref[...], taps))

    dt = jax.nn.softplus(dt_ref[...] + dtb_ref[...])
    a = dt * (-jnp.exp(alog_ref[...]))
    ri = lax.broadcasted_iota(jnp.int32, (L, L), 0)
    ci = lax.broadcasted_iota(jnp.int32, (L, L), 1)
    tril = ri >= ci
    tri_b = tril.astype(_BF16)
    a_hi, a_mid, a_lo = _split3(a)
    acum = _dot(tri_b, a_hi) + _dot(tri_b, a_mid) + _dot(tri_b, a_lo)
    acum_t = acum.T
    a_last = acum[L - 1:L, :]
    eac = jnp.exp(acum)
    decay = jnp.exp(a_last - acum)

    stack = jnp.concatenate([dt, eac, decay], axis=0)
    st_hi = stack.astype(_BF16)
    st_lo = (stack - st_hi.astype(_F32)).astype(_BF16)
    spread_ref[...] = _dot(jnp.concatenate([st_hi, st_lo], axis=1), expand_ref[...])
    dt_x = lambda cs: spread_ref[0:L, cs]
    eac_x = lambda cs: spread_ref[L:2 * L, cs]
    decay_x = lambda cs: spread_ref[2 * L:3 * L, cs]
    chunk_decay_x = lambda cs: spread_ref[2 * L - 1:2 * L, cs]

    lane = lax.broadcasted_iota(jnp.int32, (L, LANES), 1)
    lo = lane < SSM_HEADDIM

    for g in range(N_GROUPS):
        c_b = D_INNER + g * D_STATE
        c_c = D_INNER + N_GROUPS * D_STATE + g * D_STATE
        bm_b = act_ref[:, c_b:c_b + D_STATE].astype(_BF16)
        cm_b = act_ref[:, c_c:c_c + D_STATE].astype(_BF16)
        cb = lax.dot_general(cm_b, bm_b, (((1,), (1,)), ((), ())), preferred_element_type=_F32)
        gw = HEADS_PER_GROUP * SSM_HEADDIM
        gcol = g * gw
        y_off = _dot(cm_b, state_ref[:, gcol:gcol + gw].astype(_BF16))
        xdd_parts = []
        for jj in range(HEADS_PER_GROUP // 2):
            j = g * (HEADS_PER_GROUP // 2) + jj
            col = j * LANES
            cs = slice(col, col + LANES)
            xblk = act_ref[:, cs]
            xdt = xblk * dt_x(cs)
            ms = []
            for hh in range(2):
                h = 2 * j + hh
                diff = acum[:, h:h + 1] - acum_t[h:h + 1, :]
                lmat = jnp.exp(jnp.where(tril, diff, _NEG))
                ms.append((cb * lmat).astype(_BF16))
            lhs = jnp.concatenate(ms, axis=1)
            xdt_b = xdt.astype(_BF16)
            zero = jnp.zeros_like(xdt_b)
            rhs = jnp.concatenate([jnp.where(lo, xdt_b, zero), jnp.where(lo, zero, xdt_b)], axis=0)
            y = _dot(lhs, rhs)
            y = y + y_off[:, jj * LANES:(jj + 1) * LANES] * eac_x(cs)
            y = y + xblk * dsk_ref[:, cs]
            yacc_ref[:, cs] = y
            xdd_parts.append((xdt * decay_x(cs)).astype(_BF16))
        xdd = jnp.concatenate(xdd_parts, axis=1)
        upd = lax.dot_general(bm_b, xdd, (((0,), (0,)), ((), ())), preferred_element_type=_F32)
        gs = slice(gcol, gcol + gw)
        state_ref[:, gs] = state_ref[:, gs] * chunk_decay_x(gs) + upd

    for g in range(N_GROUPS):
        gw = D_INNER // N_GROUPS
        sl = slice(g * gw, (g + 1) * gw)
        yg = yacc_ref[:, sl] * _silu_of_half(z_ref[:, sl].astype(_F32))
        yg = yg * lax.rsqrt(jnp.mean(yg * yg, axis=-1, keepdims=True) + EPS) * ng_ref[:, sl]
        y_ref[:, sl] = yg.astype(y_ref.dtype)


def _ssd(proj, dt_raw, conv_w, conv_b, dt_bias, a_log, d_skip_x, norm_g, bsz, s_len):
    nc = s_len // CHUNK
    m = bsz * s_len
    full = lambda r, w: pl.BlockSpec((r, w), lambda b, c: (0, 0))
    conv_w_rows = jnp.broadcast_to(conv_w[:D_CONV - 1].astype(_BF16)[:, None, :],
                                   (D_CONV - 1, CONV_ROWS, CONV_DIM))
    return pl.pallas_call(
        _ssd_kernel,
        grid=(bsz, nc),
        in_specs=[
            pl.BlockSpec((CHUNK, CONV_DIM), lambda b, c: (b * nc + c, COL_XBC // CONV_DIM)),
            pl.BlockSpec((CHUNK, D_INNER), lambda b, c: (b * nc + c, COL_Z // D_INNER)),
            pl.BlockSpec((CHUNK, LANES), lambda b, c: (b * nc + c, 0)),
            full(D_CONV, CONV_DIM),
            pl.BlockSpec((D_CONV - 1, CONV_ROWS, CONV_DIM), lambda b, c: (0, 0, 0)),
            full(1, CONV_DIM), full(1, LANES), full(1, LANES),
            full(1, D_INNER), full(1, D_INNER),
            full(CHUNK, (D_CONV - 1) * CONV_ROWS), full(2 * LANES, D_INNER),
        ],
        out_specs=pl.BlockSpec((CHUNK, D_INNER), lambda b, c: (b * nc + c, 0)),
        out_shape=jax.ShapeDtypeStruct((m, D_INNER), _BF16),
        scratch_shapes=[
            pltpu.VMEM((D_STATE, D_INNER), _F32),
            pltpu.VMEM((CONV_HIST + CHUNK, CONV_DIM), _BF16),
            pltpu.VMEM((CHUNK, CONV_DIM), _F32),
            pltpu.VMEM((3 * CHUNK, D_INNER), _F32),
            pltpu.VMEM((CHUNK, D_INNER), _F32),
        ],
        compiler_params=pltpu.CompilerParams(
            dimension_semantics=("parallel", "arbitrary"), vmem_limit_bytes=VMEM_LIMIT),
        name="ssd",
    )(proj, proj, dt_raw, conv_w, conv_w_rows, conv_b, dt_bias, a_log, d_skip_x, norm_g,
      _shift_matrix(), _head_expand_matrix())


def _merge_kernel(x_ref, oa_ref, ys_ref, gla_ref, gls_ref, bg_ref, wao_ref, wso_ref, wo_ref, out_ref):
    y_att = _dot(oa_ref[...], wao_ref[...])
    y_ssm = _dot(ys_ref[...], wso_ref[...])
    g_a = _sigmoid(gla_ref[...].astype(_F32) + bg_ref[:, 0:D_MODEL])
    g_s = _sigmoid(gls_ref[...].astype(_F32) + bg_ref[:, D_MODEL:2 * D_MODEL])
    h = (g_a * y_att + g_s * y_ssm).astype(_BF16)
    out_ref[...] = x_ref[...] + _dot(h, wo_ref[...])


def _merge(x2, o_att, y_ssm, proj, b_gate, w_ao, w_so, w_o, tm):
    m = x2.shape[0]
    row = lambda w, cb=0: pl.BlockSpec((tm, w), lambda i: (i, cb))
    full = lambda r, w: pl.BlockSpec((r, w), lambda i: (0, 0))
    return pl.pallas_call(
        _merge_kernel,
        grid=(m // tm,),
        in_specs=[
            row(D_MODEL), row(ATT_W), row(D_INNER),
            row(D_MODEL, COL_GATE // D_MODEL), row(D_MODEL, COL_GATE // D_MODEL + 1),
            full(1, 2 * D_MODEL), full(ATT_W, D_MODEL), full(D_INNER, D_MODEL), full(D_MODEL, D_MODEL),
        ],
        out_specs=row(D_MODEL),
        out_shape=jax.ShapeDtypeStruct((m, D_MODEL), x2.dtype),
        compiler_params=pltpu.CompilerParams(
            dimension_semantics=("parallel",), vmem_limit_bytes=VMEM_LIMIT),
        name="merge",
    )(x2, o_att, y_ssm, proj, proj, b_gate, w_ao, w_so, w_o)


def _layer(x, layer_idx, norm_g, w_in, b_gate, q_norm_g, k_norm_g, lq1, lk1, lq2, lk2, subln_g,
           w_attn_out, conv_w, conv_b, dt_bias, a_log, d_skip, ssm_norm_g, w_ssm_out, w_out):
    bsz, s_len, _ = x.shape
    m = bsz * s_len
    x2 = x.reshape(m, D_MODEL)
    lam_init = 0.8 - 0.6 * math.exp(-0.3 * layer_idx)

    c_dt = COL_GATE
    w_main = jnp.concatenate([w_in[:, :COL_Z], 0.5 * w_in[:, COL_Z:COL_XBC], w_in[:, COL_XBC:c_dt],
                              w_in[:, c_dt + H_SSM:]], axis=1).astype(_BF16)
    w_dt = jnp.pad(w_in[:, c_dt:c_dt + H_SSM], ((0, 0), (0, LANES - H_SSM))).astype(_BF16)
    pad_h = lambda v: jnp.pad(v.astype(_F32), (0, LANES - H_SSM)).reshape(1, LANES)

    tm = min(2048, m)
    proj, dt_raw = _in_proj(x2, norm_g.reshape(1, D_MODEL), w_main, w_dt, tm, 1024)

    blk = min(512, s_len // 2)
    qtab, ktab = _alibi_tables(s_len)
    two = lambda v: jnp.tile(v.astype(_F32), 2).reshape(1, HEAD_W)
    qf, kf, vT = _attn_prep(proj, two(q_norm_g), two(k_norm_g), qtab, ktab, bsz, s_len, blk)
    r64 = lambda v: v.astype(_F32).reshape(1, HD)
    attn_args = (qf, kf, vT, proj, subln_g.reshape(1, HEAD_W).astype(_F32),
                 r64(lq1), r64(lk1), r64(lq2), r64(lk2))
    attn = functools.partial(_attention, bsz=bsz, s_len=s_len, blk=blk, lam_init=lam_init)
    bound = (HD ** 0.5 * LOG2E) * jnp.max(jnp.abs(q_norm_g)) * jnp.max(jnp.abs(k_norm_g))
    o_att = lax.cond(bound < SAFE_SCORE_BOUND,
                     functools.partial(attn, online=False), functools.partial(attn, online=True),
                     *attn_args)

    conv_w_half = (0.5 * conv_w).astype(_F32)
    y_ssm = _ssd(proj, dt_raw, conv_w_half, (0.5 * conv_b).reshape(1, CONV_DIM).astype(_F32),
                 pad_h(dt_bias), pad_h(a_log),
                 jnp.repeat(d_skip.astype(_F32), SSM_HEADDIM).reshape(1, D_INNER),
                 ssm_norm_g.reshape(1, D_INNER).astype(_F32), bsz, s_len)

    out = _merge(x2, o_att, y_ssm, proj, b_gate.reshape(1, 2 * D_MODEL).astype(_F32),
                 w_attn_out.astype(_BF16), w_ssm_out.astype(_BF16), w_out.astype(_BF16), min(512, m))
    return out.reshape(bsz, s_len, D_MODEL)


def kernel(x, norm_g, w_in, b_gate, q_norm_g, k_norm_g, lambda_q1, lambda_k1, lambda_q2, lambda_k2,
           subln_g, w_attn_out, conv_w, conv_b, dt_bias, a_log, d_skip, ssm_norm_g, w_ssm_out, w_out):
    h = x
    for l in range(norm_g.shape[0]):
        h = _layer(h, l, norm_g[l], w_in[l], b_gate[l], q_norm_g[l], k_norm_g[l], lambda_q1[l],
                   lambda_k1[l], lambda_q2[l], lambda_k2[l], subln_g[l], w_attn_out[l], conv_w[l],
                   conv_b[l], dt_bias[l], a_log[l], d_skip[l], ssm_norm_g[l], w_ssm_out[l], w_out[l])
    return h
```

```python
import functools
import math

import numpy as np
import jax
import jax.numpy as jnp
from jax import lax
from jax.experimental import pallas as pl
from jax.experimental.pallas import tpu as pltpu

D_MODEL = 1024
H_ATT = 8
HD = 64
HEAD_W = 2 * HD
ATT_W = H_ATT * HEAD_W
D_INNER = 2048
SSM_HEADDIM = 64
H_SSM = D_INNER // SSM_HEADDIM
N_GROUPS = 4
HEADS_PER_GROUP = H_SSM // N_GROUPS
D_STATE = 128
D_CONV = 4
CONV_DIM = D_INNER + 2 * N_GROUPS * D_STATE
CHUNK = 128
EPS = 1e-5
LANES = 128
VMEM_LIMIT = 48 * 1024 * 1024

N_QKV = 3 * ATT_W
COL_XBC = 0
COL_G = COL_XBC + CONV_DIM
COL_Z = COL_G + ATT_W
COL_GATE = COL_Z + D_INNER
N_PROJ = COL_GATE + 2 * D_MODEL
N_MAIN = N_QKV + N_PROJ

LOG2E = math.log2(math.e)
SAFE_SCORE_BOUND = 100.0

_F32 = jnp.float32
_BF16 = jnp.bfloat16
_NEG = -1e30


def _dot(a, b):
    return jnp.dot(a, b, preferred_element_type=_F32)


def _sigmoid(v):
    return 0.5 * jnp.tanh(0.5 * v) + 0.5


def _silu_of_half(h):
    return h * jnp.tanh(h) + h


def _silu(v):
    return _silu_of_half(0.5 * v)


def _in_proj_kernel(x_ref, g_ref, w_ref, wdt_ref, gq_ref, gk_ref, qtab_ref, ktab_ref,
                    proj_ref, dt_ref, qf_ref, kf_ref, vT_ref, xn_ref, *, tn):
    tm = x_ref.shape[0]
    xf = x_ref[...]
    ms = jnp.mean(xf * xf, axis=-1, keepdims=True)
    xn_ref[...] = (xf * lax.rsqrt(ms + EPS) * g_ref[...]).astype(_BF16)
    dt_ref[...] = _dot(xn_ref[...], wdt_ref[...])

    def tile(j):
        return _dot(xn_ref[...], w_ref[:, j * tn:(j + 1) * tn])

    lo = lax.broadcasted_iota(jnp.int32, (tm, HEAD_W), 1) < HD
    hi = jnp.logical_not(lo)
    q_all, k_all = tile(0), tile(1)
    for h in range(H_ATT):
        hs = slice(h * HEAD_W, (h + 1) * HEAD_W)
        qn = (_half_rms(q_all[:, hs], gq_ref[...], lo) * (HD ** -0.5 * LOG2E)).astype(_BF16)
        kn = _half_rms(k_all[:, hs], gk_ref[...], lo).astype(_BF16)
        for c, mine in enumerate((lo, hi)):
            qf_ref[h, c] = jnp.where(mine, qn, qtab_ref[h])
            kf_ref[h, c] = jnp.where(mine, kn, ktab_ref[...])
    v_all = tile(2)
    for h in range(H_ATT):
        vT_ref[h] = v_all[:, h * HEAD_W:(h + 1) * HEAD_W].T.astype(_BF16)
    for j in range(N_QKV // tn, N_MAIN // tn):
        proj_ref[:, j * tn - N_QKV:(j + 1) * tn - N_QKV] = tile(j).astype(_BF16)


def _in_proj(x2, norm_g, w_main, w_dt, gq2, gk2, qtab, ktab, bsz, s_len, tm, tn, blk):
    m = x2.shape[0]
    ns = s_len // tm
    per = blk // tm
    nk = s_len // blk
    resident = functools.partial(pl.BlockSpec, pipeline_mode=pl.Buffered(1))
    vec = lambda w: pl.BlockSpec((1, w), lambda i: (0, 0))
    return pl.pallas_call(
        functools.partial(_in_proj_kernel, tn=tn),
        grid=(m // tm,),
        in_specs=[
            pl.BlockSpec((tm, D_MODEL), lambda i: (i, 0)),
            vec(D_MODEL),
            resident((D_MODEL, N_MAIN), lambda i: (0, 0)),
            pl.BlockSpec((D_MODEL, LANES), lambda i: (0, 0)),
            vec(HEAD_W), vec(HEAD_W),
            pl.BlockSpec((H_ATT, tm, HEAD_W), lambda i: (0, i % ns, 0)),
            pl.BlockSpec((tm, HEAD_W), lambda i: (i % ns, 0)),
        ],
        out_specs=[
            pl.BlockSpec((tm, N_PROJ), lambda i: (i, 0)),
            pl.BlockSpec((tm, LANES), lambda i: (i, 0)),
            pl.BlockSpec((None, H_ATT, 2, tm, HEAD_W), lambda i: (i // ns, 0, 0, i % ns, 0)),
            pl.BlockSpec((None, H_ATT, None, 2, tm, HEAD_W),
                         lambda i: (i // ns, 0, (i % ns) // per, 0, (i % ns) % per, 0)),
            pl.BlockSpec((None, H_ATT, None, HEAD_W, tm),
                         lambda i: (i // ns, 0, (i % ns) // per, 0, (i % ns) % per)),
        ],
        out_shape=[
            jax.ShapeDtypeStruct((m, N_PROJ), _BF16),
            jax.ShapeDtypeStruct((m, LANES), _F32),
            jax.ShapeDtypeStruct((bsz, H_ATT, 2, s_len, HEAD_W), _BF16),
            jax.ShapeDtypeStruct((bsz, H_ATT, nk, 2, blk, HEAD_W), _BF16),
            jax.ShapeDtypeStruct((bsz, H_ATT, nk, HEAD_W, blk), _BF16),
        ],
        scratch_shapes=[pltpu.VMEM((tm, D_MODEL), _BF16)],
        compiler_params=pltpu.CompilerParams(
            dimension_semantics=("parallel",), vmem_limit_bytes=VMEM_LIMIT),
        name="in_proj",
    )(x2, norm_g, w_main, w_dt, gq2, gk2, qtab, ktab)


def _half_rms(v, gain2, lo):
    sq = v * v
    s_all = jnp.sum(sq, axis=-1, keepdims=True)
    s_lo = jnp.sum(jnp.where(lo, sq, 0.0), axis=-1, keepdims=True)
    ms = jnp.where(lo, s_lo, s_all - s_lo) * (1.0 / HD)
    return v * lax.rsqrt(ms + EPS) * gain2


def _bf16_terms(v, n):
    terms = []
    for _ in range(n):
        t = v.astype(_BF16).astype(_F32)
        terms.append(t)
        v = v - t
    return terms


N_FEAT = 8


def _alibi_tables(s_len):
    slopes = 2.0 ** (-8.0 * np.arange(1, H_ATT + 1) / H_ATT)
    slope2 = jnp.asarray(slopes * LOG2E, _F32)
    pos = jnp.arange(s_len, dtype=jnp.int32)
    pos_a = (pos >> 7).astype(_F32)
    pos_b = (pos & 127).astype(_F32)
    over_s = lambda v: jnp.broadcast_to(v[:, None], (H_ATT, s_len))
    q_feats = ([over_s(t) for t in _bf16_terms(slope2 * 128.0, 3) + _bf16_terms(slope2, 3)]
               + _bf16_terms(-slope2[:, None] * pos.astype(_F32)[None, :], 2))
    k_feats = [pos_a] * 3 + [pos_b] * 3 + [jnp.ones((s_len,), _F32)] * 2
    qf = jnp.stack(q_feats, axis=-1)
    kf = jnp.stack(k_feats, axis=-1)
    widen = lambda t: jnp.tile(jnp.pad(t, [(0, 0)] * (t.ndim - 1) + [(0, HD - N_FEAT)]), 2).astype(_BF16)
    return widen(qf), widen(kf)


def _attn_kernel(q_ref, kf_ref, vT_ref, g_ref, sub_ref, lq1_ref, lk1_ref, lq2_ref, lk2_ref,
                 o_ref, m_ref, l_ref, acc_ref, s_ref, *, blk, nq, lam_init, online):
    u = pl.program_id(2)
    tq = 2 * blk
    m_ref[...] = jnp.full(m_ref.shape, _NEG, _F32)
    l_ref[...] = jnp.zeros(l_ref.shape, _F32)
    acc_ref[...] = jnp.zeros(acc_ref.shape, _F32)
    nt = (((1,), (1,)), ((), ()))

    def scores(tile, j, slot, c0):
        row0 = pl.multiple_of(tile * tq + c0, blk)
        for c in range(2):
            s_ref[slot, c, :, c0:] = lax.dot_general(kf_ref[j, c], q_ref[c, pl.ds(row0, tq - c0), :], nt,
                                                     preferred_element_type=_F32)

    def masked_scores(slot, c, c0, masked):
        s = s_ref[slot, c, :, c0:]
        if masked:
            kpos = lax.broadcasted_iota(jnp.int32, s.shape, 0)
            qpos = lax.broadcasted_iota(jnp.int32, s.shape, 1)
            s = jnp.where(kpos <= qpos, s, _NEG)
        return s

    def consume(j, slot, c0, masked):
        vblk = vT_ref[j]
        for c in range(2):
            s = masked_scores(slot, c, c0, masked)
            if online:
                m_old = m_ref[c, :, c0:]
                m_new = jnp.maximum(m_old, jnp.max(s, axis=0, keepdims=True))
                alpha = jnp.exp2(m_old - m_new)
                p = jnp.exp2(s - m_new)
                l_ref[c, :, c0:] = alpha * l_ref[c, :, c0:] + jnp.sum(p, axis=0, keepdims=True)
                acc_ref[c, :, c0:] = alpha * acc_ref[c, :, c0:] + _dot(vblk, p.astype(_BF16))
                m_ref[c, :, c0:] = m_new
            else:
                p = jnp.exp2(s)
                l_ref[c, :, c0:] += jnp.sum(p, axis=0, keepdims=True)
                acc_ref[c, :, c0:] += _dot(vblk, p.astype(_BF16))

    @pl.when(u == 0)
    def _():
        scores(0, 0, 0, 0)

    def body(t, carry):
        scores(u, 2 * t + 1, 1, 0)
        consume(2 * t, 0, 0, False)
        scores(u, 2 * t + 2, 0, 0)
        consume(2 * t + 1, 1, 0, False)
        return carry

    lax.fori_loop(0, u, body, 0)
    scores(u, 2 * u + 1, 1, blk)
    consume(2 * u, 0, 0, True)
    scores(jnp.minimum(u + 1, nq - 1), 0, 0, 0)
    consume(2 * u + 1, 1, blk, True)

    lam = (jnp.exp(jnp.sum(lq1_ref[...] * lk1_ref[...], axis=-1, keepdims=True))
           - jnp.exp(jnp.sum(lq2_ref[...] * lk2_ref[...], axis=-1, keepdims=True)) + lam_init)
    oT = acc_ref[0] / l_ref[0] - lam * (acc_ref[1] / l_ref[1])
    o = oT.T
    o = o * lax.rsqrt(jnp.mean(o * o, axis=-1, keepdims=True) + EPS) * sub_ref[...] * (1.0 - lam_init)
    o_ref[...] = (o * _silu(g_ref[...].astype(_F32))).astype(_BF16)


def _attention(qf, kf, vT, proj, subln_g, lq1, lk1, lq2, lk2, *, bsz, s_len, blk, lam_init, online):
    nk = s_len // blk
    tq = 2 * blk
    nq = s_len // tq
    vec = lambda w: pl.BlockSpec((1, w), lambda b, h, i: (0, 0))
    return pl.pallas_call(
        functools.partial(_attn_kernel, blk=blk, nq=nq, lam_init=lam_init, online=online),
        grid=(bsz, H_ATT, nq),
        in_specs=[
            pl.BlockSpec((None, None, 2, s_len, HEAD_W), lambda b, h, i: (b, h, 0, 0, 0)),
            pl.BlockSpec((None, None, nk, 2, blk, HEAD_W), lambda b, h, i: (b, h, 0, 0, 0, 0)),
            pl.BlockSpec((None, None, nk, HEAD_W, blk), lambda b, h, i: (b, h, 0, 0, 0)),
            pl.BlockSpec((tq, HEAD_W), lambda b, h, i: (b * nq + i, COL_G // HEAD_W + h)),
            vec(HEAD_W), vec(HD), vec(HD), vec(HD), vec(HD),
        ],
        out_specs=pl.BlockSpec((tq, HEAD_W), lambda b, h, i: (b * nq + i, h)),
        out_shape=jax.ShapeDtypeStruct((bsz * s_len, ATT_W), _BF16),
        scratch_shapes=[
            pltpu.VMEM((2, 1, tq), _F32),
            pltpu.VMEM((2, 1, tq), _F32),
            pltpu.VMEM((2, HEAD_W, tq), _F32),
            pltpu.VMEM((2, 2, blk, tq), _F32),
        ],
        compiler_params=pltpu.CompilerParams(
            dimension_semantics=("parallel", "parallel", "arbitrary"), vmem_limit_bytes=VMEM_LIMIT),
        name="diff_attention",
    )(qf, kf, vT, proj, subln_g, lq1, lk1, lq2, lk2)


def _split3(v):
    hi = v.astype(_BF16)
    r1 = v - hi.astype(_F32)
    mid = r1.astype(_BF16)
    lo = (r1 - mid.astype(_F32)).astype(_BF16)
    return hi, mid, lo


CONV_HIST = 16
CONV_TILE = 512


CONV_ROWS = CONV_HIST + CHUNK


def _shift_matrix():
    sm = np.zeros((CHUNK, (D_CONV - 1) * CONV_ROWS), np.float32)
    t = np.arange(CHUNK)
    for j in range(D_CONV - 1):
        sm[t, j * CONV_ROWS + CONV_HIST + t - (D_CONV - 1) + j] = 1.0
    return jnp.asarray(sm, _BF16)


def _head_expand_matrix():
    em = np.zeros((2 * LANES, D_INNER), np.float32)
    cols = np.arange(D_INNER)
    em[cols // SSM_HEADDIM, cols] = 1.0
    em[LANES + cols // SSM_HEADDIM, cols] = 1.0
    return jnp.asarray(em, _BF16)


def _ssd_kernel(xbc_ref, z_ref, dt_ref, cw_ref, cwx_ref, cb_ref, dtb_ref, alog_ref, dsk_ref, ng_ref,
                shift_ref, expand_ref, y_ref, state_ref, ext_ref, act_ref, spread_ref, yacc_ref):
    L = CHUNK

    @pl.when(pl.program_id(1) == 0)
    def _():
        state_ref[...] = jnp.zeros(state_ref.shape, _F32)
        ext_ref[0:CONV_HIST, :] = jnp.zeros((CONV_HIST, CONV_DIM), _BF16)

    @pl.when(pl.program_id(1) > 0)
    def _():
        ext_ref[0:CONV_HIST, :] = ext_ref[L:L + CONV_HIST, :]

    ext_ref[CONV_HIST:, :] = xbc_ref[...]

    for ct in range(CONV_DIM // CONV_TILE):
        cs = slice(ct * CONV_TILE, (ct + 1) * CONV_TILE)
        ext = ext_ref[:, cs]
        taps = jnp.concatenate([ext * cwx_ref[t, :, cs] for t in range(D_CONV - 1)], axis=0)
        now = cb_ref[:, cs] + cw_ref[D_CONV - 1:D_CONV, cs] * ext_ref[CONV_HIST:, cs].astype(_F32)
        act_ref[:, cs] = _silu_of_half(now + _dot(shift_ref[...], taps))

    dt = jax.nn.softplus(dt_ref[...] + dtb_ref[...])
    a = dt * (-jnp.exp(alog_ref[...]))
    ri = lax.broadcasted_iota(jnp.int32, (L, L), 0)
    ci = lax.broadcasted_iota(jnp.int32, (L, L), 1)
    tril = ri >= ci
    tri_b = tril.astype(_BF16)
    a_hi, a_mid, a_lo = _split3(a)
    acum = _dot(tri_b, a_hi) + _dot(tri_b, a_mid) + _dot(tri_b, a_lo)
    acum_t = acum.T
    a_last = acum[L - 1:L, :]
    eac = jnp.exp(acum)
    decay = jnp.exp(a_last - acum)

    stack = jnp.concatenate([dt, eac, decay], axis=0)
    st_hi = stack.astype(_BF16)
    st_lo = (stack - st_hi.astype(_F32)).astype(_BF16)
    spread_ref[...] = _dot(jnp.concatenate([st_hi, st_lo], axis=1), expand_ref[...])
    dt_x = lambda cs: spread_ref[0:L, cs]
    eac_x = lambda cs: spread_ref[L:2 * L, cs]
    decay_x = lambda cs: spread_ref[2 * L:3 * L, cs]
    chunk_decay_x = lambda cs: spread_ref[2 * L - 1:2 * L, cs]

    lane = lax.broadcasted_iota(jnp.int32, (L, LANES), 1)
    lo = lane < SSM_HEADDIM

    for g in range(N_GROUPS):
        c_b = D_INNER + g * D_STATE
        c_c = D_INNER + N_GROUPS * D_STATE + g * D_STATE
        bm_b = act_ref[:, c_b:c_b + D_STATE].astype(_BF16)
        cm_b = act_ref[:, c_c:c_c + D_STATE].astype(_BF16)
        cb = lax.dot_general(cm_b, bm_b, (((1,), (1,)), ((), ())), preferred_element_type=_F32)
        gw = HEADS_PER_GROUP * SSM_HEADDIM
        gcol = g * gw
        y_off = _dot(cm_b, state_ref[:, gcol:gcol + gw].astype(_BF16))
        xdd_parts = []
        for jj in range(HEADS_PER_GROUP // 2):
            j = g * (HEADS_PER_GROUP // 2) + jj
            col = j * LANES
            cs = slice(col, col + LANES)
            xblk = act_ref[:, cs]
            xdt = xblk * dt_x(cs)
            ms = []
            for hh in range(2):
                h = 2 * j + hh
                diff = acum[:, h:h + 1] - acum_t[h:h + 1, :]
                lmat = jnp.exp(jnp.where(tril, diff, _NEG))
                ms.append((cb * lmat).astype(_BF16))
            lhs = jnp.concatenate(ms, axis=1)
            xdt_b = xdt.astype(_BF16)
            zero = jnp.zeros_like(xdt_b)
            rhs = jnp.concatenate([jnp.where(lo, xdt_b, zero), jnp.where(lo, zero, xdt_b)], axis=0)
            y = _dot(lhs, rhs)
            y = y + y_off[:, jj * LANES:(jj + 1) * LANES] * eac_x(cs)
            y = y + xblk * dsk_ref[:, cs]
            yacc_ref[:, cs] = y
            xdd_parts.append((xdt * decay_x(cs)).astype(_BF16))
        xdd = jnp.concatenate(xdd_parts, axis=1)
        upd = lax.dot_general(bm_b, xdd, (((0,), (0,)), ((), ())), preferred_element_type=_F32)
        gs = slice(gcol, gcol + gw)
        state_ref[:, gs] = state_ref[:, gs] * chunk_decay_x(gs) + upd

    for g in range(N_GROUPS):
        gw = D_INNER // N_GROUPS
        sl = slice(g * gw, (g + 1) * gw)
        yg = yacc_ref[:, sl] * _silu_of_half(z_ref[:, sl].astype(_F32))
        yg = yg * lax.rsqrt(jnp.mean(yg * yg, axis=-1, keepdims=True) + EPS) * ng_ref[:, sl]
        y_ref[:, sl] = yg.astype(y_ref.dtype)


def _ssd(proj, dt_raw, conv_w, conv_b, dt_bias, a_log, d_skip_x, norm_g, bsz, s_len):
    nc = s_len // CHUNK
    m = bsz * s_len
    full = lambda r, w: pl.BlockSpec((r, w), lambda b, c: (0, 0))
    conv_w_rows = jnp.broadcast_to(conv_w[:D_CONV - 1].astype(_BF16)[:, None, :],
                                   (D_CONV - 1, CONV_ROWS, CONV_DIM))
    return pl.pallas_call(
        _ssd_kernel,
        grid=(bsz, nc),
        in_specs=[
            pl.BlockSpec((CHUNK, CONV_DIM), lambda b, c: (b * nc + c, COL_XBC // CONV_DIM)),
            pl.BlockSpec((CHUNK, D_INNER), lambda b, c: (b * nc + c, COL_Z // D_INNER)),
            pl.BlockSpec((CHUNK, LANES), lambda b, c: (b * nc + c, 0)),
            full(D_CONV, CONV_DIM),
            pl.BlockSpec((D_CONV - 1, CONV_ROWS, CONV_DIM), lambda b, c: (0, 0, 0)),
            full(1, CONV_DIM), full(1, LANES), full(1, LANES),
            full(1, D_INNER), full(1, D_INNER),
            full(CHUNK, (D_CONV - 1) * CONV_ROWS), full(2 * LANES, D_INNER),
        ],
        out_specs=pl.BlockSpec((CHUNK, D_INNER), lambda b, c: (b * nc + c, 0)),
        out_shape=jax.ShapeDtypeStruct((m, D_INNER), _BF16),
        scratch_shapes=[
            pltpu.VMEM((D_STATE, D_INNER), _F32),
            pltpu.VMEM((CONV_HIST + CHUNK, CONV_DIM), _BF16),
            pltpu.VMEM((CHUNK, CONV_DIM), _F32),
            pltpu.VMEM((3 * CHUNK, D_INNER), _F32),
            pltpu.VMEM((CHUNK, D_INNER), _F32),
        ],
        compiler_params=pltpu.CompilerParams(
            dimension_semantics=("parallel", "arbitrary"), vmem_limit_bytes=VMEM_LIMIT),
        name="ssd",
    )(proj, proj, dt_raw, conv_w, conv_w_rows, conv_b, dt_bias, a_log, d_skip_x, norm_g,
      _shift_matrix(), _head_expand_matrix())


def _merge_kernel(x_ref, oa_ref, ys_ref, gla_ref, gls_ref, bg_ref, wao_ref, wso_ref, wo_ref, out_ref):
    y_att = _dot(oa_ref[...], wao_ref[...])
    y_ssm = _dot(ys_ref[...], wso_ref[...])
    g_a = _sigmoid(gla_ref[...].astype(_F32) + bg_ref[:, 0:D_MODEL])
    g_s = _sigmoid(gls_ref[...].astype(_F32) + bg_ref[:, D_MODEL:2 * D_MODEL])
    h = (g_a * y_att + g_s * y_ssm).astype(_BF16)
    out_ref[...] = x_ref[...] + _dot(h, wo_ref[...])


def _merge(x2, o_att, y_ssm, proj, b_gate, w_ao, w_so, w_o, tm):
    m = x2.shape[0]
    row = lambda w, cb=0: pl.BlockSpec((tm, w), lambda i: (i, cb))
    full = lambda r, w: pl.BlockSpec((r, w), lambda i: (0, 0))
    return pl.pallas_call(
        _merge_kernel,
        grid=(m // tm,),
        in_specs=[
            row(D_MODEL), row(ATT_W), row(D_INNER),
            row(D_MODEL, COL_GATE // D_MODEL), row(D_MODEL, COL_GATE // D_MODEL + 1),
            full(1, 2 * D_MODEL), full(ATT_W, D_MODEL), full(D_INNER, D_MODEL), full(D_MODEL, D_MODEL),
        ],
        out_specs=row(D_MODEL),
        out_shape=jax.ShapeDtypeStruct((m, D_MODEL), x2.dtype),
        compiler_params=pltpu.CompilerParams(
            dimension_semantics=("parallel",), vmem_limit_bytes=VMEM_LIMIT),
        name="merge",
    )(x2, o_att, y_ssm, proj, proj, b_gate, w_ao, w_so, w_o)


def _layer(x, layer_idx, norm_g, w_in, b_gate, q_norm_g, k_norm_g, lq1, lk1, lq2, lk2, subln_g,
           w_attn_out, conv_w, conv_b, dt_bias, a_log, d_skip, ssm_norm_g, w_ssm_out, w_out):
    bsz, s_len, _ = x.shape
    m = bsz * s_len
    x2 = x.reshape(m, D_MODEL)
    lam_init = 0.8 - 0.6 * math.exp(-0.3 * layer_idx)

    o_g, o_z, o_xbc = 3 * ATT_W, 4 * ATT_W, 4 * ATT_W + D_INNER
    o_dt = o_xbc + CONV_DIM
    w_main = jnp.concatenate([w_in[:, :o_g], w_in[:, o_xbc:o_dt], w_in[:, o_g:o_z],
                              0.5 * w_in[:, o_z:o_xbc], w_in[:, o_dt + H_SSM:]], axis=1).astype(_BF16)
    w_dt = jnp.pad(w_in[:, o_dt:o_dt + H_SSM], ((0, 0), (0, LANES - H_SSM))).astype(_BF16)
    pad_h = lambda v: jnp.pad(v.astype(_F32), (0, LANES - H_SSM)).reshape(1, LANES)

    blk = min(512, s_len // 2)
    qtab, ktab = _alibi_tables(s_len)
    two = lambda v: jnp.tile(v.astype(_F32), 2).reshape(1, HEAD_W)
    proj, dt_raw, qf, kf, vT = _in_proj(x2, norm_g.reshape(1, D_MODEL), w_main, w_dt, two(q_norm_g),
                                        two(k_norm_g), qtab, ktab, bsz, s_len, min(256, blk), 1024, blk)
    r64 = lambda v: v.astype(_F32).reshape(1, HD)
    attn_args = (qf, kf, vT, proj, subln_g.reshape(1, HEAD_W).astype(_F32),
                 r64(lq1), r64(lk1), r64(lq2), r64(lk2))
    attn = functools.partial(_attention, bsz=bsz, s_len=s_len, blk=blk, lam_init=lam_init)
    bound = (HD ** 0.5 * LOG2E) * jnp.max(jnp.abs(q_norm_g)) * jnp.max(jnp.abs(k_norm_g))
    o_att = lax.cond(bound < SAFE_SCORE_BOUND,
                     functools.partial(attn, online=False), functools.partial(attn, online=True),
                     *attn_args)

    conv_w_half = (0.5 * conv_w).astype(_F32)
    y_ssm = _ssd(proj, dt_raw, conv_w_half, (0.5 * conv_b).reshape(1, CONV_DIM).astype(_F32),
                 pad_h(dt_bias), pad_h(a_log),
                 jnp.repeat(d_skip.astype(_F32), SSM_HEADDIM).reshape(1, D_INNER),
                 ssm_norm_g.reshape(1, D_INNER).astype(_F32), bsz, s_len)

    out = _merge(x2, o_att, y_ssm, proj, b_gate.reshape(1, 2 * D_MODEL).astype(_F32),
                 w_attn_out.astype(_BF16), w_ssm_out.astype(_BF16), w_out.astype(_BF16), min(512, m))
    return out.reshape(bsz, s_len, D_MODEL)


def kernel(x, norm_g, w_in, b_gate, q_norm_g, k_norm_g, lambda_q1, lambda_k1, lambda_q2, lambda_k2,
           subln_g, w_attn_out, conv_w, conv_b, dt_bias, a_log, d_skip, ssm_norm_g, w_ssm_out, w_out):
    h = x
    for l in range(norm_g.shape[0]):
        h = _layer(h, l, norm_g[l], w_in[l], b_gate[l], q_norm_g[l], k_norm_g[l], lambda_q1[l],
                   lambda_k1[l], lambda_q2[l], lambda_k2[l], subln_g[l], w_attn_out[l], conv_w[l],
                   conv_b[l], dt_bias[l], a_log[l], d_skip[l], ssm_norm_g[l], w_ssm_out[l], w_out[l])
    return h
```

```python
import functools
import math

import numpy as np
import jax
import jax.numpy as jnp
from jax import lax
from jax.experimental import pallas as pl
from jax.experimental.pallas import tpu as pltpu

D_MODEL = 1024
H_ATT = 8
HD = 64
HEAD_W = 2 * HD
ATT_W = H_ATT * HEAD_W
D_INNER = 2048
SSM_HEADDIM = 64
H_SSM = D_INNER // SSM_HEADDIM
N_GROUPS = 4
HEADS_PER_GROUP = H_SSM // N_GROUPS
D_STATE = 128
D_CONV = 4
CONV_DIM = D_INNER + 2 * N_GROUPS * D_STATE
CHUNK = 128
EPS = 1e-5
LANES = 128
VMEM_LIMIT = 48 * 1024 * 1024

N_QKV = 3 * ATT_W
COL_XBC = 0
COL_G = COL_XBC + CONV_DIM
COL_Z = COL_G + ATT_W
COL_GATE = COL_Z + D_INNER
N_PROJ = COL_GATE + 2 * D_MODEL
N_MAIN = N_QKV + N_PROJ

LOG2E = math.log2(math.e)
SAFE_SCORE_BOUND = 100.0

_F32 = jnp.float32
_BF16 = jnp.bfloat16
_NEG = -1e30


def _dot(a, b):
    return jnp.dot(a, b, preferred_element_type=_F32)


def _sigmoid(v):
    return 0.5 * jnp.tanh(0.5 * v) + 0.5


def _silu_of_half(h):
    return h * jnp.tanh(h) + h


def _silu(v):
    return _silu_of_half(0.5 * v)


def _in_proj_kernel(x_ref, g_ref, w_ref, wdt_ref, gq_ref, gk_ref, qtab_ref, ktab_ref,
                    proj_ref, dt_ref, qf_ref, kf_ref, vT_ref, xn_ref, *, tn):
    tm = x_ref.shape[0]
    xf = x_ref[...]
    ms = jnp.mean(xf * xf, axis=-1, keepdims=True)
    xn_ref[...] = (xf * lax.rsqrt(ms + EPS) * g_ref[...]).astype(_BF16)
    dt_ref[...] = _dot(xn_ref[...], wdt_ref[...])

    def tile(j):
        return _dot(xn_ref[...], w_ref[:, j * tn:(j + 1) * tn])

    lo = lax.broadcasted_iota(jnp.int32, (tm, HEAD_W), 1) < HD
    hi = jnp.logical_not(lo)
    q_all, k_all = tile(0), tile(1)
    for h in range(H_ATT):
        hs = slice(h * HEAD_W, (h + 1) * HEAD_W)
        qn = (_half_rms(q_all[:, hs], gq_ref[...], lo) * (HD ** -0.5 * LOG2E)).astype(_BF16)
        kn = _half_rms(k_all[:, hs], gk_ref[...], lo).astype(_BF16)
        for c, mine in enumerate((lo, hi)):
            qf_ref[h, c] = jnp.where(mine, qn, qtab_ref[h])
            kf_ref[h, c] = jnp.where(mine, kn, ktab_ref[...])
    v_all = tile(2)
    for h in range(H_ATT):
        vT_ref[h] = v_all[:, h * HEAD_W:(h + 1) * HEAD_W].T.astype(_BF16)
    for j in range(N_QKV // tn, N_MAIN // tn):
        proj_ref[:, j * tn - N_QKV:(j + 1) * tn - N_QKV] = tile(j).astype(_BF16)


def _in_proj(x2, norm_g, w_main, w_dt, gq2, gk2, qtab, ktab, bsz, s_len, tm, tn, blk):
    m = x2.shape[0]
    ns = s_len // tm
    per = blk // tm
    nk = s_len // blk
    resident = functools.partial(pl.BlockSpec, pipeline_mode=pl.Buffered(1))
    vec = lambda w: pl.BlockSpec((1, w), lambda i: (0, 0))
    return pl.pallas_call(
        functools.partial(_in_proj_kernel, tn=tn),
        grid=(m // tm,),
        in_specs=[
            pl.BlockSpec((tm, D_MODEL), lambda i: (i, 0)),
            vec(D_MODEL),
            resident((D_MODEL, N_MAIN), lambda i: (0, 0)),
            pl.BlockSpec((D_MODEL, LANES), lambda i: (0, 0)),
            vec(HEAD_W), vec(HEAD_W),
            pl.BlockSpec((H_ATT, tm, HEAD_W), lambda i: (0, i % ns, 0)),
            pl.BlockSpec((tm, HEAD_W), lambda i: (i % ns, 0)),
        ],
        out_specs=[
            pl.BlockSpec((tm, N_PROJ), lambda i: (i, 0)),
            pl.BlockSpec((tm, LANES), lambda i: (i, 0)),
            pl.BlockSpec((None, H_ATT, 2, tm, HEAD_W), lambda i: (i // ns, 0, 0, i % ns, 0)),
            pl.BlockSpec((None, H_ATT, None, 2, tm, HEAD_W),
                         lambda i: (i // ns, 0, (i % ns) // per, 0, (i % ns) % per, 0)),
            pl.BlockSpec((None, H_ATT, None, HEAD_W, tm),
                         lambda i: (i // ns, 0, (i % ns) // per, 0, (i % ns) % per)),
        ],
        out_shape=[
            jax.ShapeDtypeStruct((m, N_PROJ), _BF16),
            jax.ShapeDtypeStruct((m, LANES), _F32),
            jax.ShapeDtypeStruct((bsz, H_ATT, 2, s_len, HEAD_W), _BF16),
            jax.ShapeDtypeStruct((bsz, H_ATT, nk, 2, blk, HEAD_W), _BF16),
            jax.ShapeDtypeStruct((bsz, H_ATT, nk, HEAD_W, blk), _BF16),
        ],
        scratch_shapes=[pltpu.VMEM((tm, D_MODEL), _BF16)],
        compiler_params=pltpu.CompilerParams(
            dimension_semantics=("parallel",), vmem_limit_bytes=VMEM_LIMIT),
        name="in_proj",
    )(x2, norm_g, w_main, w_dt, gq2, gk2, qtab, ktab)


def _half_rms(v, gain2, lo):
    sq = v * v
    s_all = jnp.sum(sq, axis=-1, keepdims=True)
    s_lo = jnp.sum(jnp.where(lo, sq, 0.0), axis=-1, keepdims=True)
    ms = jnp.where(lo, s_lo, s_all - s_lo) * (1.0 / HD)
    return v * lax.rsqrt(ms + EPS) * gain2


def _bf16_terms(v, n):
    terms = []
    for _ in range(n):
        t = v.astype(_BF16).astype(_F32)
        terms.append(t)
        v = v - t
    return terms


N_FEAT = 8


def _alibi_tables(s_len):
    slopes = 2.0 ** (-8.0 * np.arange(1, H_ATT + 1) / H_ATT)
    slope2 = jnp.asarray(slopes * LOG2E, _F32)
    pos = jnp.arange(s_len, dtype=jnp.int32)
    pos_a = (pos >> 7).astype(_F32)
    pos_b = (pos & 127).astype(_F32)
    over_s = lambda v: jnp.broadcast_to(v[:, None], (H_ATT, s_len))
    q_feats = ([over_s(t) for t in _bf16_terms(slope2 * 128.0, 3) + _bf16_terms(slope2, 3)]
               + _bf16_terms(-slope2[:, None] * pos.astype(_F32)[None, :], 2))
    k_feats = [pos_a] * 3 + [pos_b] * 3 + [jnp.ones((s_len,), _F32)] * 2
    qf = jnp.stack(q_feats, axis=-1)
    kf = jnp.stack(k_feats, axis=-1)
    widen = lambda t: jnp.tile(jnp.pad(t, [(0, 0)] * (t.ndim - 1) + [(0, HD - N_FEAT)]), 2).astype(_BF16)
    return widen(qf), widen(kf)


def _attn_kernel(q_ref, kf_ref, vT_ref, g_ref, sub_ref, lq1_ref, lk1_ref, lq2_ref, lk2_ref,
                 o_ref, m_ref, l_ref, acc_ref, s_ref, *, blk, sub, lam_init, online):
    u = pl.program_id(2)
    tq = sub * blk
    m_ref[...] = jnp.full(m_ref.shape, _NEG, _F32)
    l_ref[...] = jnp.zeros(l_ref.shape, _F32)
    acc_ref[...] = jnp.zeros(acc_ref.shape, _F32)
    nt = (((1,), (1,)), ((), ()))

    def scores(j, slot, c0):
        for c in range(2):
            s_ref[slot, c, :, c0:] = lax.dot_general(kf_ref[j, c], q_ref[c, c0:, :], nt,
                                                     preferred_element_type=_F32)

    def masked_scores(slot, c, c0, masked):
        s = s_ref[slot, c, :, c0:]
        if masked:
            kpos = lax.broadcasted_iota(jnp.int32, s.shape, 0)
            qpos = lax.broadcasted_iota(jnp.int32, s.shape, 1)
            s = jnp.where(kpos <= qpos, s, _NEG)
        return s

    def consume(j, slot, c0, masked):
        vblk = vT_ref[j]
        for c in range(2):
            s = masked_scores(slot, c, c0, masked)
            if online:
                m_old = m_ref[c, :, c0:]
                m_new = jnp.maximum(m_old, jnp.max(s, axis=0, keepdims=True))
                alpha = jnp.exp2(m_old - m_new)
                p = jnp.exp2(s - m_new)
                l_ref[c, :, c0:] = alpha * l_ref[c, :, c0:] + jnp.sum(p, axis=0, keepdims=True)
                acc_ref[c, :, c0:] = alpha * acc_ref[c, :, c0:] + _dot(vblk, p.astype(_BF16))
                m_ref[c, :, c0:] = m_new
            else:
                p = jnp.exp2(s)
                l_ref[c, :, c0:] += jnp.sum(p, axis=0, keepdims=True)
                acc_ref[c, :, c0:] += _dot(vblk, p.astype(_BF16))

    scores(0, 0, 0)

    def body(t, carry):
        scores(2 * t + 1, 1, 0)
        consume(2 * t, 0, 0, False)
        scores(2 * t + 2, 0, 0)
        consume(2 * t + 1, 1, 0, False)
        return carry

    lax.fori_loop(0, (sub // 2) * u, body, 0)
    for d in range(sub):
        if d + 1 < sub:
            scores(sub * u + d + 1, (d + 1) % 2, (d + 1) * blk)
        consume(sub * u + d, d % 2, d * blk, True)

    lam = (jnp.exp(jnp.sum(lq1_ref[...] * lk1_ref[...], axis=-1, keepdims=True))
           - jnp.exp(jnp.sum(lq2_ref[...] * lk2_ref[...], axis=-1, keepdims=True)) + lam_init)
    oT = acc_ref[0] / l_ref[0] - lam * (acc_ref[1] / l_ref[1])
    o = oT.T
    o = o * lax.rsqrt(jnp.mean(o * o, axis=-1, keepdims=True) + EPS) * sub_ref[...] * (1.0 - lam_init)
    o_ref[...] = (o * _silu(g_ref[...].astype(_F32))).astype(_BF16)


def _attention(qf, kf, vT, proj, subln_g, lq1, lk1, lq2, lk2, *, bsz, s_len, blk, lam_init, online):
    nk = s_len // blk
    sub = 4 if s_len % (4 * blk) == 0 else 2
    tq = sub * blk
    nq = s_len // tq
    vec = lambda w: pl.BlockSpec((1, w), lambda b, h, i: (0, 0))
    return pl.pallas_call(
        functools.partial(_attn_kernel, blk=blk, sub=sub, lam_init=lam_init, online=online),
        grid=(bsz, H_ATT, nq),
        in_specs=[
            pl.BlockSpec((None, None, 2, tq, HEAD_W), lambda b, h, i: (b, h, 0, i, 0)),
            pl.BlockSpec((None, None, nk, 2, blk, HEAD_W), lambda b, h, i: (b, h, 0, 0, 0, 0)),
            pl.BlockSpec((None, None, nk, HEAD_W, blk), lambda b, h, i: (b, h, 0, 0, 0)),
            pl.BlockSpec((tq, HEAD_W), lambda b, h, i: (b * nq + i, COL_G // HEAD_W + h)),
            vec(HEAD_W), vec(HD), vec(HD), vec(HD), vec(HD),
        ],
        out_specs=pl.BlockSpec((tq, HEAD_W), lambda b, h, i: (b * nq + i, h)),
        out_shape=jax.ShapeDtypeStruct((bsz * s_len, ATT_W), _BF16),
        scratch_shapes=[
            pltpu.VMEM((2, 1, tq), _F32),
            pltpu.VMEM((2, 1, tq), _F32),
            pltpu.VMEM((2, HEAD_W, tq), _F32),
            pltpu.VMEM((2, 2, blk, tq), _F32),
        ],
        compiler_params=pltpu.CompilerParams(
            dimension_semantics=("parallel", "parallel", "arbitrary"), vmem_limit_bytes=VMEM_LIMIT),
        name="diff_attention",
    )(qf, kf, vT, proj, subln_g, lq1, lk1, lq2, lk2)


def _split3(v):
    hi = v.astype(_BF16)
    r1 = v - hi.astype(_F32)
    mid = r1.astype(_BF16)
    lo = (r1 - mid.astype(_F32)).astype(_BF16)
    return hi, mid, lo


CONV_HIST = 16
CONV_TILE = 512


CONV_ROWS = CONV_HIST + CHUNK


def _shift_matrix():
    sm = np.zeros((CHUNK, (D_CONV - 1) * CONV_ROWS), np.float32)
    t = np.arange(CHUNK)
    for j in range(D_CONV - 1):
        sm[t, j * CONV_ROWS + CONV_HIST + t - (D_CONV - 1) + j] = 1.0
    return jnp.asarray(sm, _BF16)


def _head_expand_matrix():
    em = np.zeros((2 * LANES, D_INNER), np.float32)
    cols = np.arange(D_INNER)
    em[cols // SSM_HEADDIM, cols] = 1.0
    em[LANES + cols // SSM_HEADDIM, cols] = 1.0
    return jnp.asarray(em, _BF16)


def _ssd_kernel(xbc_ref, z_ref, dt_ref, cw_ref, cwx_ref, cb_ref, dtb_ref, alog_ref, dsk_ref, ng_ref,
                shift_ref, expand_ref, y_ref, state_ref, ext_ref, act_ref, spread_ref, yacc_ref):
    L = CHUNK

    @pl.when(pl.program_id(1) == 0)
    def _():
        state_ref[...] = jnp.zeros(state_ref.shape, _F32)
        ext_ref[0:CONV_HIST, :] = jnp.zeros((CONV_HIST, CONV_DIM), _BF16)

    @pl.when(pl.program_id(1) > 0)
    def _():
        ext_ref[0:CONV_HIST, :] = ext_ref[L:L + CONV_HIST, :]

    ext_ref[CONV_HIST:, :] = xbc_ref[...]

    for ct in range(CONV_DIM // CONV_TILE):
        cs = slice(ct * CONV_TILE, (ct + 1) * CONV_TILE)
        ext = ext_ref[:, cs]
        taps = jnp.concatenate([ext * cwx_ref[t, :, cs] for t in range(D_CONV - 1)], axis=0)
        now = cb_ref[:, cs] + cw_ref[D_CONV - 1:D_CONV, cs] * ext_ref[CONV_HIST:, cs].astype(_F32)
        act_ref[:, cs] = _silu_of_half(now + _dot(shift_ref[...], taps))

    dt = jax.nn.softplus(dt_ref[...] + dtb_ref[...])
    a = dt * (-jnp.exp(alog_ref[...]))
    ri = lax.broadcasted_iota(jnp.int32, (L, L), 0)
    ci = lax.broadcasted_iota(jnp.int32, (L, L), 1)
    tril = ri >= ci
    tri_b = tril.astype(_BF16)
    a_hi, a_mid, a_lo = _split3(a)
    acum = _dot(tri_b, a_hi) + _dot(tri_b, a_mid) + _dot(tri_b, a_lo)
    acum_t = acum.T
    a_last = acum[L - 1:L, :]
    eac = jnp.exp(acum)
    decay = jnp.exp(a_last - acum)

    stack = jnp.concatenate([dt, eac, decay], axis=0)
    st_hi = stack.astype(_BF16)
    st_lo = (stack - st_hi.astype(_F32)).astype(_BF16)
    spread_ref[...] = _dot(jnp.concatenate([st_hi, st_lo], axis=1), expand_ref[...])
    dt_x = lambda cs: spread_ref[0:L, cs]
    eac_x = lambda cs: spread_ref[L:2 * L, cs]
    decay_x = lambda cs: spread_ref[2 * L:3 * L, cs]
    chunk_decay_x = lambda cs: spread_ref[2 * L - 1:2 * L, cs]

    lane = lax.broadcasted_iota(jnp.int32, (L, LANES), 1)
    lo = lane < SSM_HEADDIM

    for g in range(N_GROUPS):
        c_b = D_INNER + g * D_STATE
        c_c = D_INNER + N_GROUPS * D_STATE + g * D_STATE
        bm_b = act_ref[:, c_b:c_b + D_STATE].astype(_BF16)
        cm_b = act_ref[:, c_c:c_c + D_STATE].astype(_BF16)
        cb = lax.dot_general(cm_b, bm_b, (((1,), (1,)), ((), ())), preferred_element_type=_F32)
        gw = HEADS_PER_GROUP * SSM_HEADDIM
        gcol = g * gw
        y_off = _dot(cm_b, state_ref[:, gcol:gcol + gw].astype(_BF16))
        xdd_parts = []
        for jj in range(HEADS_PER_GROUP // 2):
            j = g * (HEADS_PER_GROUP // 2) + jj
            col = j * LANES
            cs = slice(col, col + LANES)
            xblk = act_ref[:, cs]
            xdt = xblk * dt_x(cs)
            ms = []
            for hh in range(2):
                h = 2 * j + hh
                diff = acum[:, h:h + 1] - acum_t[h:h + 1, :]
                lmat = jnp.exp(jnp.where(tril, diff, _NEG))
                ms.append((cb * lmat).astype(_BF16))
            lhs = jnp.concatenate(ms, axis=1)
            xdt_b = xdt.astype(_BF16)
            zero = jnp.zeros_like(xdt_b)
            rhs = jnp.concatenate([jnp.where(lo, xdt_b, zero), jnp.where(lo, zero, xdt_b)], axis=0)
            y = _dot(lhs, rhs)
            y = y + y_off[:, jj * LANES:(jj + 1) * LANES] * eac_x(cs)
            y = y + xblk * dsk_ref[:, cs]
            yacc_ref[:, cs] = y
            xdd_parts.append((xdt * decay_x(cs)).astype(_BF16))
        xdd = jnp.concatenate(xdd_parts, axis=1)
        upd = lax.dot_general(bm_b, xdd, (((0,), (0,)), ((), ())), preferred_element_type=_F32)
        gs = slice(gcol, gcol + gw)
        state_ref[:, gs] = state_ref[:, gs] * chunk_decay_x(gs) + upd

    for g in range(N_GROUPS):
        gw = D_INNER // N_GROUPS
        sl = slice(g * gw, (g + 1) * gw)
        yg = yacc_ref[:, sl] * _silu_of_half(z_ref[:, sl].astype(_F32))
        yg = yg * lax.rsqrt(jnp.mean(yg * yg, axis=-1, keepdims=True) + EPS) * ng_ref[:, sl]
        y_ref[:, sl] = yg.astype(y_ref.dtype)


def _ssd(proj, dt_raw, conv_w, conv_b, dt_bias, a_log, d_skip_x, norm_g, bsz, s_len):
    nc = s_len // CHUNK
    m = bsz * s_len
    full = lambda r, w: pl.BlockSpec((r, w), lambda b, c: (0, 0))
    conv_w_rows = jnp.broadcast_to(conv_w[:D_CONV - 1].astype(_BF16)[:, None, :],
                                   (D_CONV - 1, CONV_ROWS, CONV_DIM))
    return pl.pallas_call(
        _ssd_kernel,
        grid=(bsz, nc),
        in_specs=[
            pl.BlockSpec((CHUNK, CONV_DIM), lambda b, c: (b * nc + c, COL_XBC // CONV_DIM)),
            pl.BlockSpec((CHUNK, D_INNER), lambda b, c: (b * nc + c, COL_Z // D_INNER)),
            pl.BlockSpec((CHUNK, LANES), lambda b, c: (b * nc + c, 0)),
            full(D_CONV, CONV_DIM),
            pl.BlockSpec((D_CONV - 1, CONV_ROWS, CONV_DIM), lambda b, c: (0, 0, 0)),
            full(1, CONV_DIM), full(1, LANES), full(1, LANES),
            full(1, D_INNER), full(1, D_INNER),
            full(CHUNK, (D_CONV - 1) * CONV_ROWS), full(2 * LANES, D_INNER),
        ],
        out_specs=pl.BlockSpec((CHUNK, D_INNER), lambda b, c: (b * nc + c, 0)),
        out_shape=jax.ShapeDtypeStruct((m, D_INNER), _BF16),
        scratch_shapes=[
            pltpu.VMEM((D_STATE, D_INNER), _F32),
            pltpu.VMEM((CONV_HIST + CHUNK, CONV_DIM), _BF16),
            pltpu.VMEM((CHUNK, CONV_DIM), _F32),
            pltpu.VMEM((3 * CHUNK, D_INNER), _F32),
            pltpu.VMEM((CHUNK, D_INNER), _F32),
        ],
        compiler_params=pltpu.CompilerParams(
            dimension_semantics=("parallel", "arbitrary"), vmem_limit_bytes=VMEM_LIMIT),
        name="ssd",
    )(proj, proj, dt_raw, conv_w, conv_w_rows, conv_b, dt_bias, a_log, d_skip_x, norm_g,
      _shift_matrix(), _head_expand_matrix())


def _merge_kernel(x_ref, oa_ref, ys_ref, gla_ref, gls_ref, bg_ref, wao_ref, wso_ref, wo_ref, out_ref):
    y_att = _dot(oa_ref[...], wao_ref[...])
    y_ssm = _dot(ys_ref[...], wso_ref[...])
    g_a = _sigmoid(gla_ref[...].astype(_F32) + bg_ref[:, 0:D_MODEL])
    g_s = _sigmoid(gls_ref[...].astype(_F32) + bg_ref[:, D_MODEL:2 * D_MODEL])
    h = (g_a * y_att + g_s * y_ssm).astype(_BF16)
    out_ref[...] = x_ref[...] + _dot(h, wo_ref[...])


def _merge(x2, o_att, y_ssm, proj, b_gate, w_ao, w_so, w_o, tm):
    m = x2.shape[0]
    row = lambda w, cb=0: pl.BlockSpec((tm, w), lambda i: (i, cb))
    full = lambda r, w: pl.BlockSpec((r, w), lambda i: (0, 0))
    return pl.pallas_call(
        _merge_kernel,
        grid=(m // tm,),
        in_specs=[
            row(D_MODEL), row(ATT_W), row(D_INNER),
            row(D_MODEL, COL_GATE // D_MODEL), row(D_MODEL, COL_GATE // D_MODEL + 1),
            full(1, 2 * D_MODEL), full(ATT_W, D_MODEL), full(D_INNER, D_MODEL), full(D_MODEL, D_MODEL),
        ],
        out_specs=row(D_MODEL),
        out_shape=jax.ShapeDtypeStruct((m, D_MODEL), x2.dtype),
        compiler_params=pltpu.CompilerParams(
            dimension_semantics=("parallel",), vmem_limit_bytes=VMEM_LIMIT),
        name="merge",
    )(x2, o_att, y_ssm, proj, proj, b_gate, w_ao, w_so, w_o)


def _layer(x, layer_idx, norm_g, w_in, b_gate, q_norm_g, k_norm_g, lq1, lk1, lq2, lk2, subln_g,
           w_attn_out, conv_w, conv_b, dt_bias, a_log, d_skip, ssm_norm_g, w_ssm_out, w_out):
    bsz, s_len, _ = x.shape
    m = bsz * s_len
    x2 = x.reshape(m, D_MODEL)
    lam_init = 0.8 - 0.6 * math.exp(-0.3 * layer_idx)

    o_g, o_z, o_xbc = 3 * ATT_W, 4 * ATT_W, 4 * ATT_W + D_INNER
    o_dt = o_xbc + CONV_DIM
    w_main = jnp.concatenate([w_in[:, :o_g], w_in[:, o_xbc:o_dt], w_in[:, o_g:o_z],
                              0.5 * w_in[:, o_z:o_xbc], w_in[:, o_dt + H_SSM:]], axis=1).astype(_BF16)
    w_dt = jnp.pad(w_in[:, o_dt:o_dt + H_SSM], ((0, 0), (0, LANES - H_SSM))).astype(_BF16)
    pad_h = lambda v: jnp.pad(v.astype(_F32), (0, LANES - H_SSM)).reshape(1, LANES)

    blk = min(512, s_len // 2)
    qtab, ktab = _alibi_tables(s_len)
    two = lambda v: jnp.tile(v.astype(_F32), 2).reshape(1, HEAD_W)
    proj, dt_raw, qf, kf, vT = _in_proj(x2, norm_g.reshape(1, D_MODEL), w_main, w_dt, two(q_norm_g),
                                        two(k_norm_g), qtab, ktab, bsz, s_len, min(256, blk), 1024, blk)
    r64 = lambda v: v.astype(_F32).reshape(1, HD)
    attn_args = (qf, kf, vT, proj, subln_g.reshape(1, HEAD_W).astype(_F32),
                 r64(lq1), r64(lk1), r64(lq2), r64(lk2))
    attn = functools.partial(_attention, bsz=bsz, s_len=s_len, blk=blk, lam_init=lam_init)
    bound = (HD ** 0.5 * LOG2E) * jnp.max(jnp.abs(q_norm_g)) * jnp.max(jnp.abs(k_norm_g))
    o_att = lax.cond(bound < SAFE_SCORE_BOUND,
                     functools.partial(attn, online=False), functools.partial(attn, online=True),
                     *attn_args)

    conv_w_half = (0.5 * conv_w).astype(_F32)
    y_ssm = _ssd(proj, dt_raw, conv_w_half, (0.5 * conv_b).reshape(1, CONV_DIM).astype(_F32),
                 pad_h(dt_bias), pad_h(a_log),
                 jnp.repeat(d_skip.astype(_F32), SSM_HEADDIM).reshape(1, D_INNER),
                 ssm_norm_g.reshape(1, D_INNER).astype(_F32), bsz, s_len)

    out = _merge(x2, o_att, y_ssm, proj, b_gate.reshape(1, 2 * D_MODEL).astype(_F32),
                 w_attn_out.astype(_BF16), w_ssm_out.astype(_BF16), w_out.astype(_BF16), min(512, m))
    return out.reshape(bsz, s_len, D_MODEL)


def kernel(x, norm_g, w_in, b_gate, q_norm_g, k_norm_g, lambda_q1, lambda_k1, lambda_q2, lambda_k2,
           subln_g, w_attn_out, conv_w, conv_b, dt_bias, a_log, d_skip, ssm_norm_g, w_ssm_out, w_out):
    h = x
    for l in range(norm_g.shape[0]):
        h = _layer(h, l, norm_g[l], w_in[l], b_gate[l], q_norm_g[l], k_norm_g[l], lambda_q1[l],
                   lambda_k1[l], lambda_q2[l], lambda_k2[l], subln_g[l], w_attn_out[l], conv_w[l],
                   conv_b[l], dt_bias[l], a_log[l], d_skip[l], ssm_norm_g[l], w_ssm_out[l], w_out[l])
    return h
```

```python
import functools
import math

import numpy as np
import jax
import jax.numpy as jnp
from jax import lax
from jax.experimental import pallas as pl
from jax.experimental.pallas import tpu as pltpu

D_MODEL = 1024
H_ATT = 8
HD = 64
HEAD_W = 2 * HD
ATT_W = H_ATT * HEAD_W
D_INNER = 2048
SSM_HEADDIM = 64
H_SSM = D_INNER // SSM_HEADDIM
N_GROUPS = 4
HEADS_PER_GROUP = H_SSM // N_GROUPS
D_STATE = 128
D_CONV = 4
CONV_DIM = D_INNER + 2 * N_GROUPS * D_STATE
CHUNK = 128
EPS = 1e-5
LANES = 128
VMEM_LIMIT = 48 * 1024 * 1024

N_QKV = 3 * ATT_W
COL_XBC = 0
COL_G = COL_XBC + CONV_DIM
COL_Z = COL_G + ATT_W
COL_GATE = COL_Z + D_INNER
N_PROJ = COL_GATE + 2 * D_MODEL
N_MAIN = N_QKV + N_PROJ

LOG2E = math.log2(math.e)
SAFE_SCORE_BOUND = 100.0

_F32 = jnp.float32
_BF16 = jnp.bfloat16
_NEG = -1e30


def _dot(a, b):
    return jnp.dot(a, b, preferred_element_type=_F32)


def _sigmoid(v):
    return 0.5 * jnp.tanh(0.5 * v) + 0.5


def _silu_of_half(h):
    return h * jnp.tanh(h) + h


def _silu(v):
    return _silu_of_half(0.5 * v)


def _in_proj_kernel(x_ref, g_ref, w_ref, wdt_ref, gq_ref, gk_ref, qtab_ref, ktab_ref,
                    proj_ref, dt_ref, qf_ref, kf_ref, vT_ref, xn_ref, *, tn):
    tm = x_ref.shape[0]
    xf = x_ref[...]
    ms = jnp.mean(xf * xf, axis=-1, keepdims=True)
    xn_ref[...] = (xf * lax.rsqrt(ms + EPS) * g_ref[...]).astype(_BF16)
    dt_ref[...] = _dot(xn_ref[...], wdt_ref[...])

    def tile(j):
        return _dot(xn_ref[...], w_ref[:, j * tn:(j + 1) * tn])

    lo = lax.broadcasted_iota(jnp.int32, (tm, HEAD_W), 1) < HD
    hi = jnp.logical_not(lo)
    q_all, k_all = tile(0), tile(1)
    for h in range(H_ATT):
        hs = slice(h * HEAD_W, (h + 1) * HEAD_W)
        qn = (_half_rms(q_all[:, hs], gq_ref[...], lo) * (HD ** -0.5 * LOG2E)).astype(_BF16)
        kn = _half_rms(k_all[:, hs], gk_ref[...], lo).astype(_BF16)
        for c, mine in enumerate((lo, hi)):
            qf_ref[h, c] = jnp.where(mine, qn, qtab_ref[h])
            kf_ref[h, c] = jnp.where(mine, kn, ktab_ref[...])
    v_all = tile(2)
    for h in range(H_ATT):
        vT_ref[h] = v_all[:, h * HEAD_W:(h + 1) * HEAD_W].T.astype(_BF16)
    for j in range(N_QKV // tn, N_MAIN // tn):
        proj_ref[:, j * tn - N_QKV:(j + 1) * tn - N_QKV] = tile(j).astype(_BF16)


def _in_proj(x2, norm_g, w_main, w_dt, gq2, gk2, qtab, ktab, bsz, s_len, tm, tn, blk):
    m = x2.shape[0]
    ns = s_len // tm
    per = blk // tm
    nk = s_len // blk
    resident = functools.partial(pl.BlockSpec, pipeline_mode=pl.Buffered(1))
    vec = lambda w: pl.BlockSpec((1, w), lambda i: (0, 0))
    return pl.pallas_call(
        functools.partial(_in_proj_kernel, tn=tn),
        grid=(m // tm,),
        in_specs=[
            pl.BlockSpec((tm, D_MODEL), lambda i: (i, 0)),
            vec(D_MODEL),
            resident((D_MODEL, N_MAIN), lambda i: (0, 0)),
            pl.BlockSpec((D_MODEL, LANES), lambda i: (0, 0)),
            vec(HEAD_W), vec(HEAD_W),
            pl.BlockSpec((H_ATT, tm, HEAD_W), lambda i: (0, i % ns, 0)),
            pl.BlockSpec((tm, HEAD_W), lambda i: (i % ns, 0)),
        ],
        out_specs=[
            pl.BlockSpec((tm, N_PROJ), lambda i: (i, 0)),
            pl.BlockSpec((tm, LANES), lambda i: (i, 0)),
            pl.BlockSpec((None, H_ATT, 2, tm, HEAD_W), lambda i: (i // ns, 0, 0, i % ns, 0)),
            pl.BlockSpec((None, H_ATT, None, 2, tm, HEAD_W),
                         lambda i: (i // ns, 0, (i % ns) // per, 0, (i % ns) % per, 0)),
            pl.BlockSpec((None, H_ATT, None, HEAD_W, tm),
                         lambda i: (i // ns, 0, (i % ns) // per, 0, (i % ns) % per)),
        ],
        out_shape=[
            jax.ShapeDtypeStruct((m, N_PROJ), _BF16),
            jax.ShapeDtypeStruct((m, LANES), _F32),
            jax.ShapeDtypeStruct((bsz, H_ATT, 2, s_len, HEAD_W), _BF16),
            jax.ShapeDtypeStruct((bsz, H_ATT, nk, 2, blk, HEAD_W), _BF16),
            jax.ShapeDtypeStruct((bsz, H_ATT, nk, HEAD_W, blk), _BF16),
        ],
        scratch_shapes=[pltpu.VMEM((tm, D_MODEL), _BF16)],
        compiler_params=pltpu.CompilerParams(
            dimension_semantics=("parallel",), vmem_limit_bytes=VMEM_LIMIT),
        name="in_proj",
    )(x2, norm_g, w_main, w_dt, gq2, gk2, qtab, ktab)


def _half_rms(v, gain2, lo):
    sq = v * v
    s_all = jnp.sum(sq, axis=-1, keepdims=True)
    s_lo = jnp.sum(jnp.where(lo, sq, 0.0), axis=-1, keepdims=True)
    ms = jnp.where(lo, s_lo, s_all - s_lo) * (1.0 / HD)
    return v * lax.rsqrt(ms + EPS) * gain2


def _bf16_terms(v, n):
    terms = []
    for _ in range(n):
        t = v.astype(_BF16).astype(_F32)
        terms.append(t)
        v = v - t
    return terms


N_FEAT = 8


def _alibi_tables(s_len):
    slopes = 2.0 ** (-8.0 * np.arange(1, H_ATT + 1) / H_ATT)
    slope2 = jnp.asarray(slopes * LOG2E, _F32)
    pos = jnp.arange(s_len, dtype=jnp.int32)
    pos_a = (pos >> 7).astype(_F32)
    pos_b = (pos & 127).astype(_F32)
    over_s = lambda v: jnp.broadcast_to(v[:, None], (H_ATT, s_len))
    q_feats = ([over_s(t) for t in _bf16_terms(slope2 * 128.0, 3) + _bf16_terms(slope2, 3)]
               + _bf16_terms(-slope2[:, None] * pos.astype(_F32)[None, :], 2))
    k_feats = [pos_a] * 3 + [pos_b] * 3 + [jnp.ones((s_len,), _F32)] * 2
    qf = jnp.stack(q_feats, axis=-1)
    kf = jnp.stack(k_feats, axis=-1)
    widen = lambda t: jnp.tile(jnp.pad(t, [(0, 0)] * (t.ndim - 1) + [(0, HD - N_FEAT)]), 2).astype(_BF16)
    return widen(qf), widen(kf)


def _attn_kernel(q_ref, kf_ref, vT_ref, g_ref, sub_ref, lq1_ref, lk1_ref, lq2_ref, lk2_ref,
                 o_ref, m_ref, l_ref, acc_ref, s_ref, *, blk, sub, lam_init, online):
    u = pl.program_id(2)
    tq = sub * blk
    m_ref[...] = jnp.full(m_ref.shape, _NEG, _F32)
    l_ref[...] = jnp.zeros(l_ref.shape, _F32)
    acc_ref[...] = jnp.zeros(acc_ref.shape, _F32)
    nt = (((1,), (1,)), ((), ()))

    def scores(j, slot, c0):
        for c in range(2):
            s_ref[slot, c, :, c0:] = lax.dot_general(kf_ref[j, c], q_ref[c, c0:, :], nt,
                                                     preferred_element_type=_F32)

    def masked_scores(slot, c, c0, masked):
        s = s_ref[slot, c, :, c0:]
        if masked:
            kpos = lax.broadcasted_iota(jnp.int32, s.shape, 0)
            qpos = lax.broadcasted_iota(jnp.int32, s.shape, 1)
            s = jnp.where(kpos <= qpos, s, _NEG)
        return s

    def consume(j, slot, c0, masked):
        vblk = vT_ref[j]
        for c in range(2):
            s = masked_scores(slot, c, c0, masked)
            if online:
                m_old = m_ref[c, :, c0:]
                m_new = jnp.maximum(m_old, jnp.max(s, axis=0, keepdims=True))
                alpha = jnp.exp2(m_old - m_new)
                p = jnp.exp2(s - m_new)
                l_ref[c, :, c0:] = alpha * l_ref[c, :, c0:] + jnp.sum(p, axis=0, keepdims=True)
                acc_ref[c, :, c0:] = alpha * acc_ref[c, :, c0:] + _dot(vblk, p.astype(_BF16))
                m_ref[c, :, c0:] = m_new
            else:
                p = jnp.exp2(s)
                l_ref[c, :, c0:] += jnp.sum(p, axis=0, keepdims=True)
                acc_ref[c, :, c0:] += _dot(vblk, p.astype(_BF16))

    scores(0, 0, 0)

    def body(t, carry):
        scores(2 * t + 1, 1, 0)
        consume(2 * t, 0, 0, False)
        scores(2 * t + 2, 0, 0)
        consume(2 * t + 1, 1, 0, False)
        return carry

    lax.fori_loop(0, (sub // 2) * u, body, 0)
    for d in range(sub):
        if d + 1 < sub:
            scores(sub * u + d + 1, (d + 1) % 2, (d + 1) * blk)
        consume(sub * u + d, d % 2, d * blk, True)

    lam = (jnp.exp(jnp.sum(lq1_ref[...] * lk1_ref[...], axis=-1, keepdims=True))
           - jnp.exp(jnp.sum(lq2_ref[...] * lk2_ref[...], axis=-1, keepdims=True)) + lam_init)
    oT = acc_ref[0] / l_ref[0] - lam * (acc_ref[1] / l_ref[1])
    o = oT.T
    o = o * lax.rsqrt(jnp.mean(o * o, axis=-1, keepdims=True) + EPS) * sub_ref[...] * (1.0 - lam_init)
    o_ref[...] = (o * _silu(g_ref[...].astype(_F32))).astype(_BF16)


def _attention(qf, kf, vT, proj, subln_g, lq1, lk1, lq2, lk2, *, bsz, s_len, blk, lam_init, online):
    nk = s_len // blk
    sub = 4 if s_len % (4 * blk) == 0 else 2
    tq = sub * blk
    nq = s_len // tq
    vec = lambda w: pl.BlockSpec((1, w), lambda b, h, i: (0, 0))
    return pl.pallas_call(
        functools.partial(_attn_kernel, blk=blk, sub=sub, lam_init=lam_init, online=online),
        grid=(bsz, H_ATT, nq),
        in_specs=[
            pl.BlockSpec((None, None, 2, tq, HEAD_W), lambda b, h, i: (b, h, 0, i, 0)),
            pl.BlockSpec((None, None, nk, 2, blk, HEAD_W), lambda b, h, i: (b, h, 0, 0, 0, 0)),
            pl.BlockSpec((None, None, nk, HEAD_W, blk), lambda b, h, i: (b, h, 0, 0, 0)),
            pl.BlockSpec((tq, HEAD_W), lambda b, h, i: (b * nq + i, COL_G // HEAD_W + h)),
            vec(HEAD_W), vec(HD), vec(HD), vec(HD), vec(HD),
        ],
        out_specs=pl.BlockSpec((tq, HEAD_W), lambda b, h, i: (b * nq + i, h)),
        out_shape=jax.ShapeDtypeStruct((bsz * s_len, ATT_W), _BF16),
        scratch_shapes=[
            pltpu.VMEM((2, 1, tq), _F32),
            pltpu.VMEM((2, 1, tq), _F32),
            pltpu.VMEM((2, HEAD_W, tq), _F32),
            pltpu.VMEM((2, 2, blk, tq), _F32),
        ],
        compiler_params=pltpu.CompilerParams(
            dimension_semantics=("parallel", "parallel", "arbitrary"), vmem_limit_bytes=VMEM_LIMIT),
        name="diff_attention",
    )(qf, kf, vT, proj, subln_g, lq1, lk1, lq2, lk2)


def _split3(v):
    hi = v.astype(_BF16)
    r1 = v - hi.astype(_F32)
    mid = r1.astype(_BF16)
    lo = (r1 - mid.astype(_F32)).astype(_BF16)
    return hi, mid, lo


CONV_HIST = 16
CONV_TILE = 512
SSD_CHUNKS_PER_STEP = 4


CONV_ROWS = CONV_HIST + CHUNK


def _shift_matrix():
    sm = np.zeros((CHUNK, (D_CONV - 1) * CONV_ROWS), np.float32)
    t = np.arange(CHUNK)
    for j in range(D_CONV - 1):
        sm[t, j * CONV_ROWS + CONV_HIST + t - (D_CONV - 1) + j] = 1.0
    return jnp.asarray(sm, _BF16)


def _head_expand_matrix():
    em = np.zeros((2 * LANES, D_INNER), np.float32)
    cols = np.arange(D_INNER)
    em[cols // SSM_HEADDIM, cols] = 1.0
    em[LANES + cols // SSM_HEADDIM, cols] = 1.0
    return jnp.asarray(em, _BF16)


def _ssd_kernel(xbc_ref, z_ref, dt_ref, cw_ref, cwx_ref, cb_ref, dtb_ref, alog_ref, dsk_ref, ng_ref,
                shift_ref, expand_ref, y_ref, state_ref, ext_ref, act_ref, spread_ref, yacc_ref):
    rows = xbc_ref.shape[0]

    @pl.when(pl.program_id(1) == 0)
    def _():
        state_ref[...] = jnp.zeros(state_ref.shape, _F32)
        ext_ref[0:CONV_HIST, :] = jnp.zeros((CONV_HIST, CONV_DIM), _BF16)

    @pl.when(pl.program_id(1) > 0)
    def _():
        ext_ref[0:CONV_HIST, :] = ext_ref[rows:rows + CONV_HIST, :]

    ext_ref[CONV_HIST:, :] = xbc_ref[...]
    for k in range(rows // CHUNK):
        _ssd_chunk(k, z_ref, dt_ref, cw_ref, cwx_ref, cb_ref, dtb_ref, alog_ref, dsk_ref, ng_ref,
                   shift_ref, expand_ref, y_ref, state_ref, ext_ref, act_ref.at[k], spread_ref.at[k],
                   yacc_ref.at[k])


def _ssd_chunk(k, z_ref, dt_ref, cw_ref, cwx_ref, cb_ref, dtb_ref, alog_ref, dsk_ref, ng_ref,
               shift_ref, expand_ref, y_ref, state_ref, ext_ref, act_ref, spread_ref, yacc_ref):
    L = CHUNK
    r0 = k * L

    for ct in range(CONV_DIM // CONV_TILE):
        cs = slice(ct * CONV_TILE, (ct + 1) * CONV_TILE)
        ext = ext_ref[r0:r0 + CONV_ROWS, cs]
        taps = jnp.concatenate([ext * cwx_ref[t, :, cs] for t in range(D_CONV - 1)], axis=0)
        cur = ext_ref[CONV_HIST + r0:CONV_HIST + r0 + L, cs].astype(_F32)
        now = cb_ref[:, cs] + cw_ref[D_CONV - 1:D_CONV, cs] * cur
        act_ref[:, cs] = _silu_of_half(now + _dot(shift_ref[...], taps))

    dt = jax.nn.softplus(dt_ref[r0:r0 + L, :] + dtb_ref[...])
    a = dt * (-jnp.exp(alog_ref[...]))
    ri = lax.broadcasted_iota(jnp.int32, (L, L), 0)
    ci = lax.broadcasted_iota(jnp.int32, (L, L), 1)
    tril = ri >= ci
    tri_b = tril.astype(_BF16)
    a_hi, a_mid, a_lo = _split3(a)
    acum = _dot(tri_b, a_hi) + _dot(tri_b, a_mid) + _dot(tri_b, a_lo)
    acum_t = acum.T
    a_last = acum[L - 1:L, :]
    eac = jnp.exp(acum)
    decay = jnp.exp(a_last - acum)

    stack = jnp.concatenate([dt, eac, decay], axis=0)
    st_hi = stack.astype(_BF16)
    st_lo = (stack - st_hi.astype(_F32)).astype(_BF16)
    spread_ref[...] = _dot(jnp.concatenate([st_hi, st_lo], axis=1), expand_ref[...])
    dt_x = lambda cs: spread_ref[0:L, cs]
    eac_x = lambda cs: spread_ref[L:2 * L, cs]
    decay_x = lambda cs: spread_ref[2 * L:3 * L, cs]
    chunk_decay_x = lambda cs: spread_ref[2 * L - 1:2 * L, cs]

    lane = lax.broadcasted_iota(jnp.int32, (L, LANES), 1)
    lo = lane < SSM_HEADDIM

    for g in range(N_GROUPS):
        c_b = D_INNER + g * D_STATE
        c_c = D_INNER + N_GROUPS * D_STATE + g * D_STATE
        bm_b = act_ref[:, c_b:c_b + D_STATE].astype(_BF16)
        cm_b = act_ref[:, c_c:c_c + D_STATE].astype(_BF16)
        cb = lax.dot_general(cm_b, bm_b, (((1,), (1,)), ((), ())), preferred_element_type=_F32)
        gw = HEADS_PER_GROUP * SSM_HEADDIM
        gcol = g * gw
        y_off = _dot(cm_b, state_ref[:, gcol:gcol + gw].astype(_BF16))
        xdd_parts = []
        for jj in range(HEADS_PER_GROUP // 2):
            j = g * (HEADS_PER_GROUP // 2) + jj
            col = j * LANES
            cs = slice(col, col + LANES)
            xblk = act_ref[:, cs]
            xdt = xblk * dt_x(cs)
            ms = []
            for hh in range(2):
                h = 2 * j + hh
                diff = acum[:, h:h + 1] - acum_t[h:h + 1, :]
                lmat = jnp.exp(jnp.where(tril, diff, _NEG))
                ms.append((cb * lmat).astype(_BF16))
            lhs = jnp.concatenate(ms, axis=1)
            xdt_b = xdt.astype(_BF16)
            zero = jnp.zeros_like(xdt_b)
            rhs = jnp.concatenate([jnp.where(lo, xdt_b, zero), jnp.where(lo, zero, xdt_b)], axis=0)
            y = _dot(lhs, rhs)
            y = y + y_off[:, jj * LANES:(jj + 1) * LANES] * eac_x(cs)
            y = y + xblk * dsk_ref[:, cs]
            yacc_ref[:, cs] = y
            xdd_parts.append((xdt * decay_x(cs)).astype(_BF16))
        xdd = jnp.concatenate(xdd_parts, axis=1)
        upd = lax.dot_general(bm_b, xdd, (((0,), (0,)), ((), ())), preferred_element_type=_F32)
        gs = slice(gcol, gcol + gw)
        state_ref[:, gs] = state_ref[:, gs] * chunk_decay_x(gs) + upd

    for g in range(N_GROUPS):
        gw = D_INNER // N_GROUPS
        sl = slice(g * gw, (g + 1) * gw)
        yg = yacc_ref[:, sl] * _silu_of_half(z_ref[r0:r0 + L, sl].astype(_F32))
        yg = yg * lax.rsqrt(jnp.mean(yg * yg, axis=-1, keepdims=True) + EPS) * ng_ref[:, sl]
        y_ref[r0:r0 + L, sl] = yg.astype(y_ref.dtype)


def _ssd(proj, dt_raw, conv_w, conv_b, dt_bias, a_log, d_skip_x, norm_g, bsz, s_len):
    cps = SSD_CHUNKS_PER_STEP if (s_len // CHUNK) % SSD_CHUNKS_PER_STEP == 0 else 1
    rows = cps * CHUNK
    nc = s_len // rows
    m = bsz * s_len
    full = lambda r, w: pl.BlockSpec((r, w), lambda b, c: (0, 0))
    conv_w_rows = jnp.broadcast_to(conv_w[:D_CONV - 1].astype(_BF16)[:, None, :],
                                   (D_CONV - 1, CONV_ROWS, CONV_DIM))
    return pl.pallas_call(
        _ssd_kernel,
        grid=(bsz, nc),
        in_specs=[
            pl.BlockSpec((rows, CONV_DIM), lambda b, c: (b * nc + c, COL_XBC // CONV_DIM)),
            pl.BlockSpec((rows, D_INNER), lambda b, c: (b * nc + c, COL_Z // D_INNER)),
            pl.BlockSpec((rows, LANES), lambda b, c: (b * nc + c, 0)),
            full(D_CONV, CONV_DIM),
            pl.BlockSpec((D_CONV - 1, CONV_ROWS, CONV_DIM), lambda b, c: (0, 0, 0)),
            full(1, CONV_DIM), full(1, LANES), full(1, LANES),
            full(1, D_INNER), full(1, D_INNER),
            full(CHUNK, (D_CONV - 1) * CONV_ROWS), full(2 * LANES, D_INNER),
        ],
        out_specs=pl.BlockSpec((rows, D_INNER), lambda b, c: (b * nc + c, 0)),
        out_shape=jax.ShapeDtypeStruct((m, D_INNER), _BF16),
        scratch_shapes=[
            pltpu.VMEM((D_STATE, D_INNER), _F32),
            pltpu.VMEM((CONV_HIST + rows, CONV_DIM), _BF16),
            pltpu.VMEM((cps, CHUNK, CONV_DIM), _F32),
            pltpu.VMEM((cps, 3 * CHUNK, D_INNER), _F32),
            pltpu.VMEM((cps, CHUNK, D_INNER), _F32),
        ],
        compiler_params=pltpu.CompilerParams(
            dimension_semantics=("parallel", "arbitrary"), vmem_limit_bytes=VMEM_LIMIT),
        name="ssd",
    )(proj, proj, dt_raw, conv_w, conv_w_rows, conv_b, dt_bias, a_log, d_skip_x, norm_g,
      _shift_matrix(), _head_expand_matrix())


def _merge_kernel(x_ref, oa_ref, ys_ref, gla_ref, gls_ref, bg_ref, wao_ref, wso_ref, wo_ref, out_ref):
    y_att = _dot(oa_ref[...], wao_ref[...])
    y_ssm = _dot(ys_ref[...], wso_ref[...])
    g_a = _sigmoid(gla_ref[...].astype(_F32) + bg_ref[:, 0:D_MODEL])
    g_s = _sigmoid(gls_ref[...].astype(_F32) + bg_ref[:, D_MODEL:2 * D_MODEL])
    h = (g_a * y_att + g_s * y_ssm).astype(_BF16)
    out_ref[...] = x_ref[...] + _dot(h, wo_ref[...])


def _merge(x2, o_att, y_ssm, proj, b_gate, w_ao, w_so, w_o, tm):
    m = x2.shape[0]
    row = lambda w, cb=0: pl.BlockSpec((tm, w), lambda i: (i, cb))
    full = lambda r, w: pl.BlockSpec((r, w), lambda i: (0, 0))
    return pl.pallas_call(
        _merge_kernel,
        grid=(m // tm,),
        in_specs=[
            row(D_MODEL), row(ATT_W), row(D_INNER),
            row(D_MODEL, COL_GATE // D_MODEL), row(D_MODEL, COL_GATE // D_MODEL + 1),
            full(1, 2 * D_MODEL), full(ATT_W, D_MODEL), full(D_INNER, D_MODEL), full(D_MODEL, D_MODEL),
        ],
        out_specs=row(D_MODEL),
        out_shape=jax.ShapeDtypeStruct((m, D_MODEL), x2.dtype),
        compiler_params=pltpu.CompilerParams(
            dimension_semantics=("parallel",), vmem_limit_bytes=VMEM_LIMIT),
        name="merge",
    )(x2, o_att, y_ssm, proj, proj, b_gate, w_ao, w_so, w_o)


def _layer(x, layer_idx, norm_g, w_in, b_gate, q_norm_g, k_norm_g, lq1, lk1, lq2, lk2, subln_g,
           w_attn_out, conv_w, conv_b, dt_bias, a_log, d_skip, ssm_norm_g, w_ssm_out, w_out):
    bsz, s_len, _ = x.shape
    m = bsz * s_len
    x2 = x.reshape(m, D_MODEL)
    lam_init = 0.8 - 0.6 * math.exp(-0.3 * layer_idx)

    o_g, o_z, o_xbc = 3 * ATT_W, 4 * ATT_W, 4 * ATT_W + D_INNER
    o_dt = o_xbc + CONV_DIM
    w_main = jnp.concatenate([w_in[:, :o_g], w_in[:, o_xbc:o_dt], w_in[:, o_g:o_z],
                              0.5 * w_in[:, o_z:o_xbc], w_in[:, o_dt + H_SSM:]], axis=1).astype(_BF16)
    w_dt = jnp.pad(w_in[:, o_dt:o_dt + H_SSM], ((0, 0), (0, LANES - H_SSM))).astype(_BF16)
    pad_h = lambda v: jnp.pad(v.astype(_F32), (0, LANES - H_SSM)).reshape(1, LANES)

    blk = min(512, s_len // 2)
    qtab, ktab = _alibi_tables(s_len)
    two = lambda v: jnp.tile(v.astype(_F32), 2).reshape(1, HEAD_W)
    proj, dt_raw, qf, kf, vT = _in_proj(x2, norm_g.reshape(1, D_MODEL), w_main, w_dt, two(q_norm_g),
                                        two(k_norm_g), qtab, ktab, bsz, s_len, min(256, blk), 1024, blk)
    r64 = lambda v: v.astype(_F32).reshape(1, HD)
    attn_args = (qf, kf, vT, proj, subln_g.reshape(1, HEAD_W).astype(_F32),
                 r64(lq1), r64(lk1), r64(lq2), r64(lk2))
    attn = functools.partial(_attention, bsz=bsz, s_len=s_len, blk=blk, lam_init=lam_init)
    bound = (HD ** 0.5 * LOG2E) * jnp.max(jnp.abs(q_norm_g)) * jnp.max(jnp.abs(k_norm_g))
    o_att = lax.cond(bound < SAFE_SCORE_BOUND,
                     functools.partial(attn, online=False), functools.partial(attn, online=True),
                     *attn_args)

    conv_w_half = (0.5 * conv_w).astype(_F32)
    y_ssm = _ssd(proj, dt_raw, conv_w_half, (0.5 * conv_b).reshape(1, CONV_DIM).astype(_F32),
                 pad_h(dt_bias), pad_h(a_log),
                 jnp.repeat(d_skip.astype(_F32), SSM_HEADDIM).reshape(1, D_INNER),
                 ssm_norm_g.reshape(1, D_INNER).astype(_F32), bsz, s_len)

    out = _merge(x2, o_att, y_ssm, proj, b_gate.reshape(1, 2 * D_MODEL).astype(_F32),
                 w_attn_out.astype(_BF16), w_ssm_out.astype(_BF16), w_out.astype(_BF16), min(512, m))
    return out.reshape(bsz, s_len, D_MODEL)


def kernel(x, norm_g, w_in, b_gate, q_norm_g, k_norm_g, lambda_q1, lambda_k1, lambda_q2, lambda_k2,
           subln_g, w_attn_out, conv_w, conv_b, dt_bias, a_log, d_skip, ssm_norm_g, w_ssm_out, w_out):
    h = x
    for l in range(norm_g.shape[0]):
        h = _layer(h, l, norm_g[l], w_in[l], b_gate[l], q_norm_g[l], k_norm_g[l], lambda_q1[l],
                   lambda_k1[l], lambda_q2[l], lambda_k2[l], subln_g[l], w_attn_out[l], conv_w[l],
                   conv_b[l], dt_bias[l], a_log[l], d_skip[l], ssm_norm_g[l], w_ssm_out[l], w_out[l])
    return h
```

```python
import functools
import math

import numpy as np
import jax
import jax.numpy as jnp
from jax import lax
from jax.experimental import pallas as pl
from jax.experimental.pallas import tpu as pltpu

D_MODEL = 1024
H_ATT = 8
HD = 64
HEAD_W = 2 * HD
ATT_W = H_ATT * HEAD_W
D_INNER = 2048
SSM_HEADDIM = 64
H_SSM = D_INNER // SSM_HEADDIM
N_GROUPS = 4
HEADS_PER_GROUP = H_SSM // N_GROUPS
D_STATE = 128
D_CONV = 4
CONV_DIM = D_INNER + 2 * N_GROUPS * D_STATE
CHUNK = 128
EPS = 1e-5
LANES = 128
VMEM_LIMIT = 48 * 1024 * 1024

N_QKV = 3 * ATT_W
COL_XBC = 0
COL_G = COL_XBC + CONV_DIM
COL_Z = COL_G + ATT_W
COL_GATE = COL_Z + D_INNER
N_PROJ = COL_GATE + 2 * D_MODEL
N_MAIN = N_QKV + N_PROJ

LOG2E = math.log2(math.e)
SAFE_SCORE_BOUND = 100.0

_F32 = jnp.float32
_BF16 = jnp.bfloat16
_NEG = -1e30


def _dot(a, b):
    return jnp.dot(a, b, preferred_element_type=_F32)


def _sigmoid(v):
    return 0.5 * jnp.tanh(0.5 * v) + 0.5


def _silu_of_half(h):
    return h * jnp.tanh(h) + h


def _silu(v):
    return _silu_of_half(0.5 * v)


def _in_proj_kernel(x_ref, g_ref, w_ref, wdt_ref, gq_ref, gk_ref, qtab_ref, ktab_ref,
                    proj_ref, dt_ref, qf_ref, kf_ref, vT_ref, xn_ref, *, tn):
    tm = x_ref.shape[0]
    xf = x_ref[...]
    ms = jnp.mean(xf * xf, axis=-1, keepdims=True)
    xn_ref[...] = (xf * lax.rsqrt(ms + EPS) * g_ref[...]).astype(_BF16)
    dt_ref[...] = _dot(xn_ref[...], wdt_ref[...])

    def tile(j):
        return _dot(xn_ref[...], w_ref[:, j * tn:(j + 1) * tn])

    lo = lax.broadcasted_iota(jnp.int32, (tm, HEAD_W), 1) < HD
    hi = jnp.logical_not(lo)
    q_all, k_all = tile(0), tile(1)
    for h in range(H_ATT):
        hs = slice(h * HEAD_W, (h + 1) * HEAD_W)
        qn = (_half_rms(q_all[:, hs], gq_ref[...], lo) * (HD ** -0.5 * LOG2E)).astype(_BF16)
        kn = _half_rms(k_all[:, hs], gk_ref[...], lo).astype(_BF16)
        for c, mine in enumerate((lo, hi)):
            qf_ref[h, c] = jnp.where(mine, qn, qtab_ref[h])
            kf_ref[h, c] = jnp.where(mine, kn, ktab_ref[...])
    v_all = tile(2)
    for h in range(H_ATT):
        vT_ref[h] = v_all[:, h * HEAD_W:(h + 1) * HEAD_W].T.astype(_BF16)
    for j in range(N_QKV // tn, N_MAIN // tn):
        proj_ref[:, j * tn - N_QKV:(j + 1) * tn - N_QKV] = tile(j).astype(_BF16)


def _in_proj(x2, norm_g, w_main, w_dt, gq2, gk2, qtab, ktab, bsz, s_len, tm, tn, blk):
    m = x2.shape[0]
    ns = s_len // tm
    per = blk // tm
    nk = s_len // blk
    resident = functools.partial(pl.BlockSpec, pipeline_mode=pl.Buffered(1))
    vec = lambda w: pl.BlockSpec((1, w), lambda i: (0, 0))
    return pl.pallas_call(
        functools.partial(_in_proj_kernel, tn=tn),
        grid=(m // tm,),
        in_specs=[
            pl.BlockSpec((tm, D_MODEL), lambda i: (i, 0)),
            vec(D_MODEL),
            resident((D_MODEL, N_MAIN), lambda i: (0, 0)),
            pl.BlockSpec((D_MODEL, LANES), lambda i: (0, 0)),
            vec(HEAD_W), vec(HEAD_W),
            pl.BlockSpec((H_ATT, tm, HEAD_W), lambda i: (0, i % ns, 0)),
            pl.BlockSpec((tm, HEAD_W), lambda i: (i % ns, 0)),
        ],
        out_specs=[
            pl.BlockSpec((tm, N_PROJ), lambda i: (i, 0)),
            pl.BlockSpec((tm, LANES), lambda i: (i, 0)),
            pl.BlockSpec((None, H_ATT, 2, tm, HEAD_W), lambda i: (i // ns, 0, 0, i % ns, 0)),
            pl.BlockSpec((None, H_ATT, None, 2, tm, HEAD_W),
                         lambda i: (i // ns, 0, (i % ns) // per, 0, (i % ns) % per, 0)),
            pl.BlockSpec((None, H_ATT, None, HEAD_W, tm),
                         lambda i: (i // ns, 0, (i % ns) // per, 0, (i % ns) % per)),
        ],
        out_shape=[
            jax.ShapeDtypeStruct((m, N_PROJ), _BF16),
            jax.ShapeDtypeStruct((m, LANES), _F32),
            jax.ShapeDtypeStruct((bsz, H_ATT, 2, s_len, HEAD_W), _BF16),
            jax.ShapeDtypeStruct((bsz, H_ATT, nk, 2, blk, HEAD_W), _BF16),
            jax.ShapeDtypeStruct((bsz, H_ATT, nk, HEAD_W, blk), _BF16),
        ],
        scratch_shapes=[pltpu.VMEM((tm, D_MODEL), _BF16)],
        compiler_params=pltpu.CompilerParams(
            dimension_semantics=("parallel",), vmem_limit_bytes=VMEM_LIMIT),
        name="in_proj",
    )(x2, norm_g, w_main, w_dt, gq2, gk2, qtab, ktab)


def _half_rms(v, gain2, lo):
    sq = v * v
    s_all = jnp.sum(sq, axis=-1, keepdims=True)
    s_lo = jnp.sum(jnp.where(lo, sq, 0.0), axis=-1, keepdims=True)
    ms = jnp.where(lo, s_lo, s_all - s_lo) * (1.0 / HD)
    return v * lax.rsqrt(ms + EPS) * gain2


def _bf16_terms(v, n):
    terms = []
    for _ in range(n):
        t = v.astype(_BF16).astype(_F32)
        terms.append(t)
        v = v - t
    return terms


N_FEAT = 8


def _alibi_tables(s_len):
    slopes = 2.0 ** (-8.0 * np.arange(1, H_ATT + 1) / H_ATT)
    slope2 = jnp.asarray(slopes * LOG2E, _F32)
    pos = jnp.arange(s_len, dtype=jnp.int32)
    pos_a = (pos >> 7).astype(_F32)
    pos_b = (pos & 127).astype(_F32)
    over_s = lambda v: jnp.broadcast_to(v[:, None], (H_ATT, s_len))
    q_feats = ([over_s(t) for t in _bf16_terms(slope2 * 128.0, 3) + _bf16_terms(slope2, 3)]
               + _bf16_terms(-slope2[:, None] * pos.astype(_F32)[None, :], 2))
    k_feats = [pos_a] * 3 + [pos_b] * 3 + [jnp.ones((s_len,), _F32)] * 2
    qf = jnp.stack(q_feats, axis=-1)
    kf = jnp.stack(k_feats, axis=-1)
    widen = lambda t: jnp.tile(jnp.pad(t, [(0, 0)] * (t.ndim - 1) + [(0, HD - N_FEAT)]), 2).astype(_BF16)
    return widen(qf), widen(kf)


def _attn_kernel(q_ref, kf_ref, vT_ref, g_ref, sub_ref, lq1_ref, lk1_ref, lq2_ref, lk2_ref,
                 o_ref, m_ref, l_ref, acc_ref, s_ref, *, blk, sub, lam_init, online):
    u = pl.program_id(2)
    tq = sub * blk
    m_ref[...] = jnp.full(m_ref.shape, _NEG, _F32)
    l_ref[...] = jnp.zeros(l_ref.shape, _F32)
    acc_ref[...] = jnp.zeros(acc_ref.shape, _F32)
    nt = (((1,), (1,)), ((), ()))

    def scores(j, slot, c0):
        for c in range(2):
            s_ref[slot, c, :, c0:] = lax.dot_general(kf_ref[j, c], q_ref[c, c0:, :], nt,
                                                     preferred_element_type=_F32)

    def masked_scores(slot, c, c0, masked):
        s = s_ref[slot, c, :, c0:]
        if masked:
            kpos = lax.broadcasted_iota(jnp.int32, s.shape, 0)
            qpos = lax.broadcasted_iota(jnp.int32, s.shape, 1)
            s = jnp.where(kpos <= qpos, s, _NEG)
        return s

    def consume(j, slot, c0, masked):
        vblk = vT_ref[j]
        for c in range(2):
            s = masked_scores(slot, c, c0, masked)
            if online:
                m_old = m_ref[c, :, c0:]
                m_new = jnp.maximum(m_old, jnp.max(s, axis=0, keepdims=True))
                alpha = jnp.exp2(m_old - m_new)
                p = jnp.exp2(s - m_new)
                l_ref[c, :, c0:] = alpha * l_ref[c, :, c0:] + jnp.sum(p, axis=0, keepdims=True)
                acc_ref[c, :, c0:] = alpha * acc_ref[c, :, c0:] + _dot(vblk, p.astype(_BF16))
                m_ref[c, :, c0:] = m_new
            else:
                p = jnp.exp2(s)
                l_ref[c, :, c0:] += jnp.sum(p, axis=0, keepdims=True)
                acc_ref[c, :, c0:] += _dot(vblk, p.astype(_BF16))

    scores(0, 0, 0)

    per_trip = 2 if online else sub

    def body(t, carry):
        for i in range(per_trip):
            scores(per_trip * t + i + 1, (i + 1) % 2, 0)
            consume(per_trip * t + i, i % 2, 0, False)
        return carry

    lax.fori_loop(0, (sub // per_trip) * u, body, 0)
    for d in range(sub):
        if d + 1 < sub:
            scores(sub * u + d + 1, (d + 1) % 2, (d + 1) * blk)
        consume(sub * u + d, d % 2, d * blk, True)

    lam = (jnp.exp(jnp.sum(lq1_ref[...] * lk1_ref[...], axis=-1, keepdims=True))
           - jnp.exp(jnp.sum(lq2_ref[...] * lk2_ref[...], axis=-1, keepdims=True)) + lam_init)
    oT = acc_ref[0] / l_ref[0] - lam * (acc_ref[1] / l_ref[1])
    o = oT.T
    o = o * lax.rsqrt(jnp.mean(o * o, axis=-1, keepdims=True) + EPS) * sub_ref[...] * (1.0 - lam_init)
    o_ref[...] = (o * _silu(g_ref[...].astype(_F32))).astype(_BF16)


def _attention(qf, kf, vT, proj, subln_g, lq1, lk1, lq2, lk2, *, bsz, s_len, blk, lam_init, online):
    nk = s_len // blk
    sub = 4 if s_len % (4 * blk) == 0 else 2
    tq = sub * blk
    nq = s_len // tq
    vec = lambda w: pl.BlockSpec((1, w), lambda b, h, i: (0, 0))
    return pl.pallas_call(
        functools.partial(_attn_kernel, blk=blk, sub=sub, lam_init=lam_init, online=online),
        grid=(bsz, H_ATT, nq),
        in_specs=[
            pl.BlockSpec((None, None, 2, tq, HEAD_W), lambda b, h, i: (b, h, 0, i, 0)),
            pl.BlockSpec((None, None, nk, 2, blk, HEAD_W), lambda b, h, i: (b, h, 0, 0, 0, 0)),
            pl.BlockSpec((None, None, nk, HEAD_W, blk), lambda b, h, i: (b, h, 0, 0, 0)),
            pl.BlockSpec((tq, HEAD_W), lambda b, h, i: (b * nq + i, COL_G // HEAD_W + h)),
            vec(HEAD_W), vec(HD), vec(HD), vec(HD), vec(HD),
        ],
        out_specs=pl.BlockSpec((tq, HEAD_W), lambda b, h, i: (b * nq + i, h)),
        out_shape=jax.ShapeDtypeStruct((bsz * s_len, ATT_W), _BF16),
        scratch_shapes=[
            pltpu.VMEM((2, 1, tq), _F32),
            pltpu.VMEM((2, 1, tq), _F32),
            pltpu.VMEM((2, HEAD_W, tq), _F32),
            pltpu.VMEM((2, 2, blk, tq), _F32),
        ],
        compiler_params=pltpu.CompilerParams(
            dimension_semantics=("parallel", "parallel", "arbitrary"), vmem_limit_bytes=VMEM_LIMIT),
        name="diff_attention",
    )(qf, kf, vT, proj, subln_g, lq1, lk1, lq2, lk2)


def _split3(v):
    hi = v.astype(_BF16)
    r1 = v - hi.astype(_F32)
    mid = r1.astype(_BF16)
    lo = (r1 - mid.astype(_F32)).astype(_BF16)
    return hi, mid, lo


CONV_HIST = 16
CONV_TILE = 512
SSD_CHUNKS_PER_STEP = 4


CONV_ROWS = CONV_HIST + CHUNK


def _shift_matrix():
    sm = np.zeros((CHUNK, (D_CONV - 1) * CONV_ROWS), np.float32)
    t = np.arange(CHUNK)
    for j in range(D_CONV - 1):
        sm[t, j * CONV_ROWS + CONV_HIST + t - (D_CONV - 1) + j] = 1.0
    return jnp.asarray(sm, _BF16)


def _head_expand_matrix():
    em = np.zeros((2 * LANES, D_INNER), np.float32)
    cols = np.arange(D_INNER)
    em[cols // SSM_HEADDIM, cols] = 1.0
    em[LANES + cols // SSM_HEADDIM, cols] = 1.0
    return jnp.asarray(em, _BF16)


def _ssd_kernel(xbc_ref, z_ref, dt_ref, cw_ref, cwx_ref, cb_ref, dtb_ref, alog_ref, dsk_ref, ng_ref,
                shift_ref, expand_ref, y_ref, state_ref, ext_ref, act_ref, spread_ref, yacc_ref):
    rows = xbc_ref.shape[0]

    @pl.when(pl.program_id(1) == 0)
    def _():
        state_ref[...] = jnp.zeros(state_ref.shape, _F32)
        ext_ref[0:CONV_HIST, :] = jnp.zeros((CONV_HIST, CONV_DIM), _BF16)

    @pl.when(pl.program_id(1) > 0)
    def _():
        ext_ref[0:CONV_HIST, :] = ext_ref[rows:rows + CONV_HIST, :]

    ext_ref[CONV_HIST:, :] = xbc_ref[...]
    for k in range(rows // CHUNK):
        _ssd_chunk(k, z_ref, dt_ref, cw_ref, cwx_ref, cb_ref, dtb_ref, alog_ref, dsk_ref, ng_ref,
                   shift_ref, expand_ref, y_ref, state_ref, ext_ref, act_ref.at[k], spread_ref.at[k],
                   yacc_ref.at[k])


def _ssd_chunk(k, z_ref, dt_ref, cw_ref, cwx_ref, cb_ref, dtb_ref, alog_ref, dsk_ref, ng_ref,
               shift_ref, expand_ref, y_ref, state_ref, ext_ref, act_ref, spread_ref, yacc_ref):
    L = CHUNK
    r0 = k * L

    for ct in range(CONV_DIM // CONV_TILE):
        cs = slice(ct * CONV_TILE, (ct + 1) * CONV_TILE)
        ext = ext_ref[r0:r0 + CONV_ROWS, cs]
        taps = jnp.concatenate([ext * cwx_ref[t, :, cs] for t in range(D_CONV - 1)], axis=0)
        cur = ext_ref[CONV_HIST + r0:CONV_HIST + r0 + L, cs].astype(_F32)
        now = cb_ref[:, cs] + cw_ref[D_CONV - 1:D_CONV, cs] * cur
        act_ref[:, cs] = _silu_of_half(now + _dot(shift_ref[...], taps))

    dt = jax.nn.softplus(dt_ref[r0:r0 + L, :] + dtb_ref[...])
    a = dt * (-jnp.exp(alog_ref[...]))
    ri = lax.broadcasted_iota(jnp.int32, (L, L), 0)
    ci = lax.broadcasted_iota(jnp.int32, (L, L), 1)
    tril = ri >= ci
    tri_b = tril.astype(_BF16)
    a_hi, a_mid, a_lo = _split3(a)
    acum = _dot(tri_b, a_hi) + _dot(tri_b, a_mid) + _dot(tri_b, a_lo)
    acum_t = acum.T
    a_last = acum[L - 1:L, :]
    eac = jnp.exp(acum)
    decay = jnp.exp(a_last - acum)

    stack = jnp.concatenate([dt, eac, decay], axis=0)
    st_hi = stack.astype(_BF16)
    st_lo = (stack - st_hi.astype(_F32)).astype(_BF16)
    spread_ref[...] = _dot(jnp.concatenate([st_hi, st_lo], axis=1), expand_ref[...])
    dt_x = lambda cs: spread_ref[0:L, cs]
    eac_x = lambda cs: spread_ref[L:2 * L, cs]
    decay_x = lambda cs: spread_ref[2 * L:3 * L, cs]
    chunk_decay_x = lambda cs: spread_ref[2 * L - 1:2 * L, cs]

    lane = lax.broadcasted_iota(jnp.int32, (L, LANES), 1)
    lo = lane < SSM_HEADDIM

    for g in range(N_GROUPS):
        c_b = D_INNER + g * D_STATE
        c_c = D_INNER + N_GROUPS * D_STATE + g * D_STATE
        bm_b = act_ref[:, c_b:c_b + D_STATE].astype(_BF16)
        cm_b = act_ref[:, c_c:c_c + D_STATE].astype(_BF16)
        cb = lax.dot_general(cm_b, bm_b, (((1,), (1,)), ((), ())), preferred_element_type=_F32)
        gw = HEADS_PER_GROUP * SSM_HEADDIM
        gcol = g * gw
        y_off = _dot(cm_b, state_ref[:, gcol:gcol + gw].astype(_BF16))
        xdd_parts = []
        for jj in range(HEADS_PER_GROUP // 2):
            j = g * (HEADS_PER_GROUP // 2) + jj
            col = j * LANES
            cs = slice(col, col + LANES)
            xblk = act_ref[:, cs]
            xdt = xblk * dt_x(cs)
            ms = []
            for hh in range(2):
                h = 2 * j + hh
                diff = acum[:, h:h + 1] - acum_t[h:h + 1, :]
                lmat = jnp.exp(jnp.where(tril, diff, _NEG))
                ms.append((cb * lmat).astype(_BF16))
            lhs = jnp.concatenate(ms, axis=1)
            xdt_b = xdt.astype(_BF16)
            zero = jnp.zeros_like(xdt_b)
            rhs = jnp.concatenate([jnp.where(lo, xdt_b, zero), jnp.where(lo, zero, xdt_b)], axis=0)
            y = _dot(lhs, rhs)
            y = y + y_off[:, jj * LANES:(jj + 1) * LANES] * eac_x(cs)
            y = y + xblk * dsk_ref[:, cs]
            yacc_ref[:, cs] = y
            xdd_parts.append((xdt * decay_x(cs)).astype(_BF16))
        xdd = jnp.concatenate(xdd_parts, axis=1)
        upd = lax.dot_general(bm_b, xdd, (((0,), (0,)), ((), ())), preferred_element_type=_F32)
        gs = slice(gcol, gcol + gw)
        state_ref[:, gs] = state_ref[:, gs] * chunk_decay_x(gs) + upd

    for g in range(N_GROUPS):
        gw = D_INNER // N_GROUPS
        sl = slice(g * gw, (g + 1) * gw)
        yg = yacc_ref[:, sl] * _silu_of_half(z_ref[r0:r0 + L, sl].astype(_F32))
        yg = yg * lax.rsqrt(jnp.mean(yg * yg, axis=-1, keepdims=True) + EPS) * ng_ref[:, sl]
        y_ref[r0:r0 + L, sl] = yg.astype(y_ref.dtype)


def _ssd(proj, dt_raw, conv_w, conv_b, dt_bias, a_log, d_skip_x, norm_g, bsz, s_len):
    cps = SSD_CHUNKS_PER_STEP if (s_len // CHUNK) % SSD_CHUNKS_PER_STEP == 0 else 1
    rows = cps * CHUNK
    nc = s_len // rows
    m = bsz * s_len
    full = lambda r, w: pl.BlockSpec((r, w), lambda b, c: (0, 0))
    conv_w_rows = jnp.broadcast_to(conv_w[:D_CONV - 1].astype(_BF16)[:, None, :],
                                   (D_CONV - 1, CONV_ROWS, CONV_DIM))
    return pl.pallas_call(
        _ssd_kernel,
        grid=(bsz, nc),
        in_specs=[
            pl.BlockSpec((rows, CONV_DIM), lambda b, c: (b * nc + c, COL_XBC // CONV_DIM)),
            pl.BlockSpec((rows, D_INNER), lambda b, c: (b * nc + c, COL_Z // D_INNER)),
            pl.BlockSpec((rows, LANES), lambda b, c: (b * nc + c, 0)),
            full(D_CONV, CONV_DIM),
            pl.BlockSpec((D_CONV - 1, CONV_ROWS, CONV_DIM), lambda b, c: (0, 0, 0)),
            full(1, CONV_DIM), full(1, LANES), full(1, LANES),
            full(1, D_INNER), full(1, D_INNER),
            full(CHUNK, (D_CONV - 1) * CONV_ROWS), full(2 * LANES, D_INNER),
        ],
        out_specs=pl.BlockSpec((rows, D_INNER), lambda b, c: (b * nc + c, 0)),
        out_shape=jax.ShapeDtypeStruct((m, D_INNER), _BF16),
        scratch_shapes=[
            pltpu.VMEM((D_STATE, D_INNER), _F32),
            pltpu.VMEM((CONV_HIST + rows, CONV_DIM), _BF16),
            pltpu.VMEM((cps, CHUNK, CONV_DIM), _F32),
            pltpu.VMEM((cps, 3 * CHUNK, D_INNER), _F32),
            pltpu.VMEM((cps, CHUNK, D_INNER), _F32),
        ],
        compiler_params=pltpu.CompilerParams(
            dimension_semantics=("parallel", "arbitrary"), vmem_limit_bytes=VMEM_LIMIT),
        name="ssd",
    )(proj, proj, dt_raw, conv_w, conv_w_rows, conv_b, dt_bias, a_log, d_skip_x, norm_g,
      _shift_matrix(), _head_expand_matrix())


def _merge_kernel(x_ref, oa_ref, ys_ref, gla_ref, gls_ref, bg_ref, wao_ref, wso_ref, wo_ref, out_ref):
    y_att = _dot(oa_ref[...], wao_ref[...])
    y_ssm = _dot(ys_ref[...], wso_ref[...])
    g_a = _sigmoid(gla_ref[...].astype(_F32) + bg_ref[:, 0:D_MODEL])
    g_s = _sigmoid(gls_ref[...].astype(_F32) + bg_ref[:, D_MODEL:2 * D_MODEL])
    h = (g_a * y_att + g_s * y_ssm).astype(_BF16)
    out_ref[...] = x_ref[...] + _dot(h, wo_ref[...])


def _merge(x2, o_att, y_ssm, proj, b_gate, w_ao, w_so, w_o, tm):
    m = x2.shape[0]
    row = lambda w, cb=0: pl.BlockSpec((tm, w), lambda i: (i, cb))
    full = lambda r, w: pl.BlockSpec((r, w), lambda i: (0, 0))
    return pl.pallas_call(
        _merge_kernel,
        grid=(m // tm,),
        in_specs=[
            row(D_MODEL), row(ATT_W), row(D_INNER),
            row(D_MODEL, COL_GATE // D_MODEL), row(D_MODEL, COL_GATE // D_MODEL + 1),
            full(1, 2 * D_MODEL), full(ATT_W, D_MODEL), full(D_INNER, D_MODEL), full(D_MODEL, D_MODEL),
        ],
        out_specs=row(D_MODEL),
        out_shape=jax.ShapeDtypeStruct((m, D_MODEL), x2.dtype),
        compiler_params=pltpu.CompilerParams(
            dimension_semantics=("parallel",), vmem_limit_bytes=VMEM_LIMIT),
        name="merge",
    )(x2, o_att, y_ssm, proj, proj, b_gate, w_ao, w_so, w_o)


def _layer(x, layer_idx, norm_g, w_in, b_gate, q_norm_g, k_norm_g, lq1, lk1, lq2, lk2, subln_g,
           w_attn_out, conv_w, conv_b, dt_bias, a_log, d_skip, ssm_norm_g, w_ssm_out, w_out):
    bsz, s_len, _ = x.shape
    m = bsz * s_len
    x2 = x.reshape(m, D_MODEL)
    lam_init = 0.8 - 0.6 * math.exp(-0.3 * layer_idx)

    o_g, o_z, o_xbc = 3 * ATT_W, 4 * ATT_W, 4 * ATT_W + D_INNER
    o_dt = o_xbc + CONV_DIM
    w_main = jnp.concatenate([w_in[:, :o_g], w_in[:, o_xbc:o_dt], w_in[:, o_g:o_z],
                              0.5 * w_in[:, o_z:o_xbc], w_in[:, o_dt + H_SSM:]], axis=1).astype(_BF16)
    w_dt = jnp.pad(w_in[:, o_dt:o_dt + H_SSM], ((0, 0), (0, LANES - H_SSM))).astype(_BF16)
    pad_h = lambda v: jnp.pad(v.astype(_F32), (0, LANES - H_SSM)).reshape(1, LANES)

    blk = min(512, s_len // 2)
    qtab, ktab = _alibi_tables(s_len)
    two = lambda v: jnp.tile(v.astype(_F32), 2).reshape(1, HEAD_W)
    proj, dt_raw, qf, kf, vT = _in_proj(x2, norm_g.reshape(1, D_MODEL), w_main, w_dt, two(q_norm_g),
                                        two(k_norm_g), qtab, ktab, bsz, s_len, min(256, blk), 1024, blk)
    r64 = lambda v: v.astype(_F32).reshape(1, HD)
    attn_args = (qf, kf, vT, proj, subln_g.reshape(1, HEAD_W).astype(_F32),
                 r64(lq1), r64(lk1), r64(lq2), r64(lk2))
    attn = functools.partial(_attention, bsz=bsz, s_len=s_len, blk=blk, lam_init=lam_init)
    bound = (HD ** 0.5 * LOG2E) * jnp.max(jnp.abs(q_norm_g)) * jnp.max(jnp.abs(k_norm_g))
    o_att = lax.cond(bound < SAFE_SCORE_BOUND,
                     functools.partial(attn, online=False), functools.partial(attn, online=True),
                     *attn_args)

    conv_w_half = (0.5 * conv_w).astype(_F32)
    y_ssm = _ssd(proj, dt_raw, conv_w_half, (0.5 * conv_b).reshape(1, CONV_DIM).astype(_F32),
                 pad_h(dt_bias), pad_h(a_log),
                 jnp.repeat(d_skip.astype(_F32), SSM_HEADDIM).reshape(1, D_INNER),
                 ssm_norm_g.reshape(1, D_INNER).astype(_F32), bsz, s_len)

    out = _merge(x2, o_att, y_ssm, proj, b_gate.reshape(1, 2 * D_MODEL).astype(_F32),
                 w_attn_out.astype(_BF16), w_ssm_out.astype(_BF16), w_out.astype(_BF16), min(512, m))
    return out.reshape(bsz, s_len, D_MODEL)


def kernel(x, norm_g, w_in, b_gate, q_norm_g, k_norm_g, lambda_q1, lambda_k1, lambda_q2, lambda_k2,
           subln_g, w_attn_out, conv_w, conv_b, dt_bias, a_log, d_skip, ssm_norm_g, w_ssm_out, w_out):
    h = x
    for l in range(norm_g.shape[0]):
        h = _layer(h, l, norm_g[l], w_in[l], b_gate[l], q_norm_g[l], k_norm_g[l], lambda_q1[l],
                   lambda_k1[l], lambda_q2[l], lambda_k2[l], subln_g[l], w_attn_out[l], conv_w[l],
                   conv_b[l], dt_bias[l], a_log[l], d_skip[l], ssm_norm_g[l], w_ssm_out[l], w_out[l])
    return h
```
